```python
import math
import jax
import jax.numpy as jnp
from jax import lax
import numpy as np

D_MODEL = 1024
BATCH = 2
SEQ = 16384
DEPTH = 1

HEAD_DIM = 64
DIFF_WIDTH = D_MODEL // 2
DIFF_V_DIM = 2 * HEAD_DIM
N_DIFF_HEADS = DIFF_WIDTH // DIFF_V_DIM
MOBA_WIDTH = D_MODEL - DIFF_WIDTH
N_MOBA_HEADS = MOBA_WIDTH // HEAD_DIM
MIX_WIDTH = DIFF_WIDTH + MOBA_WIDTH
DQ_COLS = N_DIFF_HEADS * 2 * HEAD_DIM
DK_COLS = N_DIFF_HEADS * 2 * HEAD_DIM
DV_COLS = N_DIFF_HEADS * DIFF_V_DIM
MQ_COLS = N_MOBA_HEADS * HEAD_DIM
MK_COLS = N_MOBA_HEADS * HEAD_DIM
MV_COLS = N_MOBA_HEADS * HEAD_DIM
IN_WIDTH = DQ_COLS + DK_COLS + DV_COLS + MQ_COLS + MK_COLS + MV_COLS

Q_BLOCK = 128
MOBA_Q_BLOCK = 64
MOBA_BLOCK = 256
MOBA_TOPK = 3

N_BUCKETS = 32
MAX_DISTANCE = 128
N_BIAS_HEADS = N_DIFF_HEADS + N_MOBA_HEADS

N_GROUPS = 4
EXPERTS_PER_GROUP = 8
N_EXPERTS = N_GROUPS * EXPERTS_PER_GROUP
EXPERT_TOPK = 2
EXPERT_FF = D_MODEL // 2

EPS = 1e-6
NEG = -1e30

kernel_name = "hymba_diff_moba_hmoe_block"


def rmsnorm(x, gain):
    xf = x.astype(jnp.float32)
    y = xf * lax.rsqrt(jnp.mean(xf * xf, axis=-1, keepdims=True) + EPS)
    return (y * gain.astype(jnp.float32)).astype(x.dtype)


def rel_bucket(dist):
    n = jnp.maximum(dist, 0)
    max_exact = N_BUCKETS // 2
    nf = jnp.maximum(n, 1).astype(jnp.float32)
    large = max_exact + (jnp.log(nf / max_exact) / math.log(MAX_DISTANCE / max_exact)
                         * (N_BUCKETS - max_exact)).astype(jnp.int32)
    large = jnp.minimum(large, N_BUCKETS - 1)
    return jnp.where(n < max_exact, n, large)


def diff_attention(q, k, v, lam, bias_table):
    b, h, _, s, dh = q.shape
    nqb = s // Q_BLOCK
    qb = q.reshape(b, h, 2, nqb, Q_BLOCK, dh).transpose(3, 0, 1, 2, 4, 5)
    k_pos = jnp.arange(s)
    scale = dh ** -0.5

    def block(args):
        qi, i = args
        q_pos = i * Q_BLOCK + jnp.arange(Q_BLOCK)
        dist = q_pos[:, None] - k_pos[None, :]
        bias = jnp.moveaxis(bias_table[rel_bucket(dist)], -1, 0).astype(jnp.float32)
        sc = jnp.einsum('bhmqd,bhmkd->bhmqk', qi, k).astype(jnp.float32) * scale
        sc = sc + bias[None, :, None]
        sc = jnp.where(dist >= 0, sc, NEG)
        p = jax.nn.softmax(sc, axis=-1)
        a = p[:, :, 0] - lam * p[:, :, 1]
        return jnp.einsum('bhqk,bhkd->bhqd', a.astype(v.dtype), v)

    out = lax.map(block, (qb, jnp.arange(nqb)))
    return out.transpose(1, 2, 0, 3, 4).reshape(b, h, s, v.shape[-1])


def moba_attention(q, k, v, bias_table):
    b, h, s, dh = q.shape
    nkb = -(-s // MOBA_BLOCK)
    s_pad = nkb * MOBA_BLOCK
    pad = ((0, 0), (0, 0), (0, s_pad - s), (0, 0))
    k_p = jnp.pad(k, pad)
    v_p = jnp.pad(v, pad)
    kb = k_p.reshape(b, h, nkb, MOBA_BLOCK, dh)
    vb = v_p.reshape(b, h, nkb, MOBA_BLOCK, dh)
    k_mean = jnp.mean(kb.astype(jnp.float32), axis=3)
    topk = min(MOBA_TOPK, nkb)
    nqb = s // MOBA_Q_BLOCK
    qb = q.reshape(b, h, nqb, MOBA_Q_BLOCK, dh).transpose(2, 0, 1, 3, 4)
    blk_pos = jnp.arange(MOBA_BLOCK)
    tab_t = bias_table.T.astype(jnp.float32)
    h_idx = jnp.arange(h)[None, :, None, None, None]
    scale = dh ** -0.5
    gather = jax.vmap(jax.vmap(lambda blocks, idx: blocks[idx]))

    def block(args):
        qi, i = args
        q_pos = i * MOBA_Q_BLOCK + jnp.arange(MOBA_Q_BLOCK)
        own = (i * MOBA_Q_BLOCK) // MOBA_BLOCK
        g = jnp.einsum('bhqd,bhnd->bhqn', qi.astype(jnp.float32), k_mean)
        g = jnp.where(jnp.arange(nkb) < own, g, -jnp.inf)
        _, sel = lax.top_k(g, topk)
        valid = sel < own
        ks = gather(kb, sel)
        vs = gather(vb, sel)
        kpos_sel = sel[..., None] * MOBA_BLOCK + blk_pos
        dist_sel = q_pos[None, None, :, None, None] - kpos_sel
        s_sel = jnp.einsum('bhqd,bhqjkd->bhqjk', qi, ks).astype(jnp.float32) * scale
        s_sel = s_sel + tab_t[h_idx, rel_bucket(dist_sel)]
        s_sel = jnp.where(valid[..., None], s_sel, NEG)
        k_own = lax.dynamic_slice_in_dim(k_p, own * MOBA_BLOCK, MOBA_BLOCK, axis=2)
        v_own = lax.dynamic_slice_in_dim(v_p, own * MOBA_BLOCK, MOBA_BLOCK, axis=2)
        dist_own = q_pos[:, None] - (own * MOBA_BLOCK + blk_pos)[None, :]
        bias_own = jnp.moveaxis(bias_table[rel_bucket(dist_own)], -1, 0).astype(jnp.float32)
        s_own = jnp.einsum('bhqd,bhkd->bhqk', qi, k_own).astype(jnp.float32) * scale
        s_own = jnp.where(dist_own >= 0, s_own + bias_own[None], NEG)
        n_sel = topk * MOBA_BLOCK
        sc = jnp.concatenate([s_sel.reshape(b, h, MOBA_Q_BLOCK, n_sel), s_own], axis=-1)
        p = jax.nn.softmax(sc, axis=-1).astype(v.dtype)
        p_sel = p[..., :n_sel].reshape(b, h, MOBA_Q_BLOCK, topk, MOBA_BLOCK)
        p_own = p[..., n_sel:]
        return (jnp.einsum('bhqjk,bhqjkd->bhqd', p_sel, vs)
                + jnp.einsum('bhqk,bhkd->bhqd', p_own, v_own))

    out = lax.map(block, (qb, jnp.arange(nqb)))
    return out.transpose(1, 2, 0, 3, 4).reshape(b, h, s, dh)


def token_mixer(h, w_in, lq1, lk1, lq2, lk2, head_gain, w_out, rel_bias, lambda_init):
    b, s, _ = h.shape
    proj = h @ w_in
    splits = np.cumsum([DQ_COLS, DK_COLS, DV_COLS, MQ_COLS, MK_COLS]).tolist()
    dq, dk, dv, mq, mk, mv = jnp.split(proj, splits, axis=-1)
    dq = dq.reshape(b, s, N_DIFF_HEADS, 2, HEAD_DIM).transpose(0, 2, 3, 1, 4)
    dk = dk.reshape(b, s, N_DIFF_HEADS, 2, HEAD_DIM).transpose(0, 2, 3, 1, 4)
    dv = dv.reshape(b, s, N_DIFF_HEADS, DIFF_V_DIM).transpose(0, 2, 1, 3)
    mq = mq.reshape(b, s, N_MOBA_HEADS, HEAD_DIM).transpose(0, 2, 1, 3)
    mk = mk.reshape(b, s, N_MOBA_HEADS, HEAD_DIM).transpose(0, 2, 1, 3)
    mv = mv.reshape(b, s, N_MOBA_HEADS, HEAD_DIM).transpose(0, 2, 1, 3)

    lam = (jnp.exp(jnp.sum(lq1.astype(jnp.float32) * lk1.astype(jnp.float32)))
           - jnp.exp(jnp.sum(lq2.astype(jnp.float32) * lk2.astype(jnp.float32)))
           + lambda_init)
    y_diff = diff_attention(dq, dk, dv, lam, rel_bias[:, :N_DIFF_HEADS])
    y_diff = rmsnorm(y_diff, head_gain) * (1.0 - lambda_init)
    y_moba = moba_attention(mq, mk, mv, rel_bias[:, N_DIFF_HEADS:])

    y = jnp.concatenate([y_diff.transpose(0, 2, 1, 3).reshape(b, s, DIFF_WIDTH),
                         y_moba.transpose(0, 2, 1, 3).reshape(b, s, MOBA_WIDTH)], axis=-1)
    return y @ w_out


def hier_moe(h, w_group, b_group, w_expert, b_expert, w_gate, w_up, w_down):
    n = h.shape[0]
    rows = jnp.arange(n)
    gl = (h @ w_group + b_group).astype(jnp.float32)
    gp = jax.nn.softmax(gl, axis=-1)
    g_idx = jnp.argmax(gl, axis=-1)
    g_w = gp[rows, g_idx]
    el = (h @ w_expert + b_expert).astype(jnp.float32).reshape(n, N_GROUPS, EXPERTS_PER_GROUP)
    ep = jax.nn.softmax(el[rows, g_idx], axis=-1)
    tv, ti = lax.top_k(ep, EXPERT_TOPK)
    tv = tv / jnp.sum(tv, axis=-1, keepdims=True)
    wts = g_w[:, None] * tv
    e_idx = g_idx[:, None] * EXPERTS_PER_GROUP + ti
    gates = jnp.sum(jax.nn.one_hot(e_idx, N_EXPERTS, dtype=jnp.float32) * wts[..., None], axis=1)
    gates = gates.astype(h.dtype)

    def expert(acc, prm):
        wg, wu, wd, gt = prm
        hid = jax.nn.silu(h @ wg) * (h @ wu)
        return acc + gt[:, None] * (hid @ wd), None

    out, _ = lax.scan(expert, jnp.zeros_like(h), (w_gate, w_up, w_down, gates.T))
    return out


def setup_inputs(seed: int = 0) -> dict:
    key = jax.random.key(seed)
    ks = jax.random.split(key, 24)
    f32 = jnp.float32
    nrm = lambda k, shape, s: jax.random.normal(k, shape, f32) * s
    gain = lambda k, shape: 1.0 + 0.05 * jax.random.normal(k, shape, f32)
    d = D_MODEL
    return {
        "x": nrm(ks[0], (BATCH, SEQ, d), 1.0),
        "c": nrm(ks[1], (BATCH, d), 1.0),
        "w_ada": nrm(ks[2], (DEPTH, d, 6 * d), 0.5 * d ** -0.5),
        "b_ada": nrm(ks[3], (DEPTH, 6 * d), 0.01),
        "pre_mix_gain": gain(ks[4], (DEPTH, d)),
        "post_mix_gain": gain(ks[5], (DEPTH, d)),
        "pre_ffn_gain": gain(ks[6], (DEPTH, d)),
        "post_ffn_gain": gain(ks[7], (DEPTH, d)),
        "w_in": nrm(ks[8], (DEPTH, d, IN_WIDTH), d ** -0.5),
        "lambda_q1": nrm(ks[9], (DEPTH, HEAD_DIM), 0.1),
        "lambda_k1": nrm(ks[10], (DEPTH, HEAD_DIM), 0.1),
        "lambda_q2": nrm(ks[11], (DEPTH, HEAD_DIM), 0.1),
        "lambda_k2": nrm(ks[12], (DEPTH, HEAD_DIM), 0.1),
        "diff_head_gain": gain(ks[13], (DEPTH, DIFF_V_DIM)),
        "w_out": nrm(ks[14], (DEPTH, MIX_WIDTH, d), MIX_WIDTH ** -0.5),
        "rel_bias": nrm(ks[15], (N_BUCKETS, N_BIAS_HEADS), 0.5),
        "w_group": nrm(ks[16], (DEPTH, d, N_GROUPS), d ** -0.5),
        "b_group": nrm(ks[17], (DEPTH, N_GROUPS), 0.01),
        "w_expert": nrm(ks[18], (DEPTH, d, N_EXPERTS), d ** -0.5),
        "b_expert": nrm(ks[19], (DEPTH, N_EXPERTS), 0.01),
        "w_gate": nrm(ks[20], (DEPTH, N_EXPERTS, d, EXPERT_FF), d ** -0.5),
        "w_up": nrm(ks[21], (DEPTH, N_EXPERTS, d, EXPERT_FF), d ** -0.5),
        "w_down": nrm(ks[22], (DEPTH, N_EXPERTS, EXPERT_FF, d), EXPERT_FF ** -0.5),
    }


def reference(x, c, w_ada, b_ada, pre_mix_gain, post_mix_gain, pre_ffn_gain, post_ffn_gain,
              w_in, lambda_q1, lambda_k1, lambda_q2, lambda_k2, diff_head_gain, w_out,
              rel_bias, w_group, b_group, w_expert, b_expert, w_gate, w_up, w_down):
    b, s, d = x.shape
    for l in range(DEPTH):
        lambda_init = 0.8 - 0.6 * math.exp(-0.3 * l)
        mod = jax.nn.silu(c) @ w_ada[l] + b_ada[l]
        sh1, sc1, g1, sh2, sc2, g2 = jnp.split(mod, 6, axis=-1)
        h = rmsnorm(x, pre_mix_gain[l]) * (1.0 + sc1[:, None]) + sh1[:, None]
        y = token_mixer(h, w_in[l], lambda_q1[l], lambda_k1[l], lambda_q2[l], lambda_k2[l],
                        diff_head_gain[l], w_out[l], rel_bias, lambda_init)
        x = x + g1[:, None] * rmsnorm(y, post_mix_gain[l])
        h = rmsnorm(x, pre_ffn_gain[l]) * (1.0 + sc2[:, None]) + sh2[:, None]
        y = hier_moe(h.reshape(b * s, d), w_group[l], b_group[l], w_expert[l], b_expert[l],
                     w_gate[l], w_up[l], w_down[l]).reshape(b, s, d)
        x = x + g2[:, None] * rmsnorm(y, post_ffn_gain[l])
    return x
```

```python
import functools
import math

import jax
import jax.numpy as jnp
from jax import lax
from jax.experimental import pallas as pl
from jax.experimental.pallas import tpu as pltpu

F32 = jnp.float32
BF16 = jnp.bfloat16
HIGHEST = lax.Precision.HIGHEST

D_MODEL = 1024
HEAD_DIM = 64
N_DIFF_HEADS = 4
N_MOBA_HEADS = 8
DIFF_COLS = 512
MOBA_COLS = 512
MOBA_BLOCK = 256
MOBA_TOPK = 3
N_BUCKETS = 32
MAX_EXACT = N_BUCKETS // 2
MAX_DISTANCE = 128
N_GROUPS = 4
EXPERTS_PER_GROUP = 8
N_EXPERTS = 32
EXPERT_FF = 512
EPS = 1e-6
NEG = -1e30
LANES = 128

IN_TILE = 512
DIFF_TILE = 512
MID_TILE = 512
MOE_TILE = 1024
VMEM_LIMIT = 56 * 1024 * 1024


def _params(n_axes, vmem=VMEM_LIMIT):
    return pltpu.CompilerParams(dimension_semantics=("arbitrary",) * n_axes,
                                vmem_limit_bytes=vmem)


def _rms(v):
    return v * lax.rsqrt(jnp.mean(v * v, axis=-1, keepdims=True) + EPS)


def _ada_kernel(c_ref, w_ref, b_ref, o_ref):
    c = c_ref[...]
    s = c * jax.nn.sigmoid(c)
    o_ref[...] = jnp.dot(s, w_ref[...], precision=HIGHEST,
                         preferred_element_type=F32) + b_ref[...]


def _ada(c_pad, w_ada, b_ada):
    n = w_ada.shape[1]
    return pl.pallas_call(
        _ada_kernel,
        grid=(n // D_MODEL,),
        in_specs=[pl.BlockSpec((8, D_MODEL), lambda j: (0, 0)),
                  pl.BlockSpec((D_MODEL, D_MODEL), lambda j: (0, j)),
                  pl.BlockSpec((1, D_MODEL), lambda j: (0, j))],
        out_specs=pl.BlockSpec((8, D_MODEL), lambda j: (0, j)),
        out_shape=jax.ShapeDtypeStruct((8, n), F32),
        compiler_params=_params(1),
        name="ada_mod",
    )(c_pad, w_ada, b_ada)


def _bias_kernel(tab_ref, o_ref, *, tile, head0):
    w = pl.program_id(0)
    h = pl.program_id(1) + head0
    r = lax.broadcasted_iota(jnp.int32, (tile, tile), 0)
    c = lax.broadcasted_iota(jnp.int32, (tile, tile), 1)
    d = r - c + w * tile
    n = jnp.maximum(d, 0)
    nf = jnp.maximum(n, 1).astype(F32)
    large = MAX_EXACT + (jnp.log(nf / MAX_EXACT) / math.log(MAX_DISTANCE / MAX_EXACT)
                         * (N_BUCKETS - MAX_EXACT)).astype(jnp.int32)
    large = jnp.minimum(large, N_BUCKETS - 1)
    bucket = jnp.where(n < MAX_EXACT, n, large)
    far = tab_ref[h, N_BUCKETS - 1]
    val = jnp.zeros((tile, tile), F32)
    for b in range(N_BUCKETS - 1):
        val = jnp.where(bucket == b, tab_ref[h, b] - far, val)
    o_ref[0, 0] = jnp.where(d >= 0, val, NEG)


def _bias_tiles(tab_t, tile, head0, n_heads):
    return pl.pallas_call(
        functools.partial(_bias_kernel, tile=tile, head0=head0),
        grid=(2, n_heads),
        in_specs=[pl.BlockSpec(memory_space=pltpu.SMEM)],
        out_specs=pl.BlockSpec((1, 1, tile, tile), lambda w, h: (w, h, 0, 0)),
        out_shape=jax.ShapeDtypeStruct((2, n_heads, tile, tile), F32),
        compiler_params=_params(2),
        name="bias_tiles",
    )(tab_t)


W_DQ, W_DK, W_DV = 0, 512, 1024
W_MQ, W_MK, W_MV = 1536, 2560, 3584
W_END = 4096


def _inproj_kernel(x_ref, gain_ref, sc_ref, sh_ref, w_ref,
                   dq_ref, dk_ref, dv_ref, mq_ref, mk_ref, mv_ref, km_ref):
    i = pl.program_id(1)
    h = _rms(x_ref[0]) * gain_ref[...]
    h = h * (1.0 + sc_ref[0]) + sh_ref[0]
    hb = h.astype(BF16)

    def proj(lo, hi):
        return jnp.dot(hb, w_ref[:, lo:hi], preferred_element_type=F32)

    dq_ref[0] = proj(W_DQ, W_DK).astype(BF16)
    dk_ref[0] = proj(W_DK, W_DV).astype(BF16)
    dv_ref[0] = proj(W_DV, W_MQ).astype(BF16)
    mq_ref[0] = proj(W_MQ, W_MK).astype(BF16)
    mv_ref[0] = proj(W_MV, W_END).astype(BF16)
    mk = proj(W_MK, W_MV)
    for r in range(IN_TILE // MOBA_BLOCK):
        km_ref[0, r] = jnp.mean(mk[r * MOBA_BLOCK:(r + 1) * MOBA_BLOCK], axis=0, keepdims=True)
    row = lax.broadcasted_iota(jnp.int32, mk.shape, 0)
    lane = lax.broadcasted_iota(jnp.int32, mk.shape, 1)
    blk = (i * IN_TILE + row) // MOBA_BLOCK
    onehot = (lane % LANES) - HEAD_DIM == blk
    mk_ref[0] = jnp.where(onehot, 1.0, mk).astype(BF16)


def _inproj(x, gain, sc1, sh1, w_aug):
    b, s, d = x.shape
    nkb = s // MOBA_BLOCK
    tok = lambda width: pl.BlockSpec((1, IN_TILE, width), lambda bi, i: (bi, i, 0))
    vec = pl.BlockSpec((1, 1, d), lambda bi, i: (bi, 0, 0))
    shp = lambda width: jax.ShapeDtypeStruct((b, s, width), BF16)
    return pl.pallas_call(
        _inproj_kernel,
        grid=(b, s // IN_TILE),
        in_specs=[tok(d), pl.BlockSpec((1, d), lambda bi, i: (0, 0)), vec, vec,
                  pl.BlockSpec((d, W_END), lambda bi, i: (0, 0))],
        out_specs=[tok(512), tok(512), tok(512), tok(1024), tok(1024), tok(512),
                   pl.BlockSpec((1, IN_TILE // MOBA_BLOCK, 1, 1024), lambda bi, i: (bi, i, 0, 0))],
        out_shape=[shp(512), shp(512), shp(512), shp(1024), shp(1024), shp(512),
                   jax.ShapeDtypeStruct((b, nkb, 1, 1024), F32)],
        compiler_params=_params(2),
        name="in_proj",
    )(x, gain, sc1, sh1, w_aug)


def _softmax_step(s, v, m_ref, l_ref, acc_ref, idx):
    m_prev = m_ref[idx]
    m_next = jnp.maximum(m_prev, jnp.max(s, axis=1, keepdims=True))
    p = jnp.exp(s - m_next)
    alpha = jnp.exp(m_prev - m_next)
    l_ref[idx] = alpha * l_ref[idx] + jnp.sum(p, axis=1, keepdims=True)
    acc_ref[idx] = alpha * acc_ref[idx] + jnp.dot(p.astype(BF16), v, preferred_element_type=F32)
    m_ref[idx] = m_next


def _nt_dot(a, b):
    return lax.dot_general(a, b, (((1,), (1,)), ((), ())), preferred_element_type=F32)


def _diff_kernel(lq1_ref, lk1_ref, lq2_ref, lk2_ref, q_ref, k_ref, v_ref, bias_ref, gain_ref,
                 o_ref, qs, m_s, l_s, acc_s, *, lambda_init):
    i = pl.program_id(2)
    t = DIFF_TILE
    q = q_ref[0]
    lane = lax.broadcasted_iota(jnp.int32, q.shape, 1)
    zero = jnp.zeros_like(q)
    qs[0] = jnp.where(lane < HEAD_DIM, q, zero)
    qs[1] = jnp.where(lane >= HEAD_DIM, q, zero)
    m_s[...] = jnp.full(m_s.shape, -jnp.inf, F32)
    l_s[...] = jnp.zeros(l_s.shape, F32)
    acc_s[...] = jnp.zeros(acc_s.shape, F32)

    def step(j, bias):
        off = pl.multiple_of(j * t, t)
        k = k_ref[0, pl.ds(off, t), :]
        v = v_ref[0, pl.ds(off, t), :]
        for m in range(2):
            s = _nt_dot(qs[m], k)
            if bias is not None:
                s = s + bias
            _softmax_step(s, v, m_s, l_s, acc_s, m)

    step(i, bias_ref[0, 0])

    @pl.when(i > 0)
    def _():
        step(i - 1, bias_ref[1, 0])

    def far(j, carry):
        step(j, None)
        return carry

    lax.fori_loop(0, i - 1, far, 0)

    lam = (jnp.exp(jnp.sum(lq1_ref[...] * lk1_ref[...], axis=-1, keepdims=True))
           - jnp.exp(jnp.sum(lq2_ref[...] * lk2_ref[...], axis=-1, keepdims=True)) + lambda_init)
    a = acc_s[0] / l_s[0] - lam * (acc_s[1] / l_s[1])
    y = _rms(a) * gain_ref[...] * (1.0 - lambda_init)
    o_ref[0] = y.astype(BF16)


def _diff_attention(dq, dk, dv, bias, lq1, lk1, lq2, lk2, head_gain, lambda_init):
    b, s, _ = dq.shape
    t = DIFF_TILE
    lam_spec = pl.BlockSpec((1, HEAD_DIM), lambda bi, h, i: (0, 0))
    full = pl.BlockSpec((1, s, LANES), lambda bi, h, i: (bi, 0, h))
    return pl.pallas_call(
        functools.partial(_diff_kernel, lambda_init=lambda_init),
        grid=(b, N_DIFF_HEADS, s // t),
        in_specs=[lam_spec, lam_spec, lam_spec, lam_spec,
                  pl.BlockSpec((1, t, LANES), lambda bi, h, i: (bi, i, h)),
                  full, full,
                  pl.BlockSpec((2, 1, t, t), lambda bi, h, i: (0, h, 0, 0)),
                  pl.BlockSpec((1, LANES), lambda bi, h, i: (0, 0))],
        out_specs=pl.BlockSpec((1, t, LANES), lambda bi, h, i: (bi, i, h)),
        out_shape=jax.ShapeDtypeStruct((b, s, DIFF_COLS), BF16),
        scratch_shapes=[pltpu.VMEM((2, t, LANES), BF16),
                        pltpu.VMEM((2, t, 1), F32),
                        pltpu.VMEM((2, t, 1), F32),
                        pltpu.VMEM((2, t, LANES), F32)],
        compiler_params=_params(3),
        name="diff_attn",
    )(lq1, lk1, lq2, lk2, dq, dk, dv, bias, head_gain)


def _moba_kernel(q_ref, k_ref, v_ref, km_ref, bias_ref, o_ref, qs, m_s, l_s, acc_s):
    i = pl.program_id(2)
    t = MOBA_BLOCK
    lane = lax.broadcasted_iota(jnp.int32, (t, LANES), 1)
    blk = lane - HEAD_DIM
    for hh in range(2):
        qa = q_ref[0, :, hh * LANES:(hh + 1) * LANES]
        g = jnp.dot(qa.astype(F32), km_ref[0, hh], precision=HIGHEST,
                    preferred_element_type=F32)
        g = jnp.where((blk >= 0) & (blk < i), g, -jnp.inf)
        sel = blk == i
        for _ in range(MOBA_TOPK):
            mx = jnp.max(g, axis=1, keepdims=True)
            first = jnp.min(jnp.where(g == mx, lane, 2 * LANES), axis=1, keepdims=True)
            pick = (lane == first) & (mx > -jnp.inf)
            sel = sel | pick
            g = jnp.where(pick, -jnp.inf, g)
        gate = jnp.where(sel, 0.0, NEG).astype(BF16)
        qs[hh] = jnp.where(lane < HEAD_DIM, qa, gate)
    m_s[...] = jnp.full(m_s.shape, -jnp.inf, F32)
    l_s[...] = jnp.zeros(l_s.shape, F32)
    acc_s[...] = jnp.zeros(acc_s.shape, F32)

    def step(j, which):
        off = pl.multiple_of(j * t, t)
        k = k_ref[0, pl.ds(off, t), :]
        v = v_ref[0, pl.ds(off, t), :]
        for hh in range(2):
            s = _nt_dot(qs[hh], k[:, hh * LANES:(hh + 1) * LANES])
            if which is not None:
                s = s + bias_ref[which, hh]
            _softmax_step(s, v, m_s, l_s, acc_s, hh)

    step(i, 0)

    @pl.when(i > 0)
    def _():
        step(i - 1, 1)

    def far(j, carry):
        step(j, None)
        return carry

    lax.fori_loop(0, i - 1, far, 0)

    o = jnp.where(lane < HEAD_DIM, acc_s[0] / l_s[0], acc_s[1] / l_s[1])
    o_ref[0] = o.astype(BF16)


def _moba_attention(mq, mk, mv, km, bias):
    b, s, _ = mq.shape
    t = MOBA_BLOCK
    return pl.pallas_call(
        _moba_kernel,
        grid=(b, N_MOBA_HEADS // 2, s // t),
        in_specs=[pl.BlockSpec((1, t, 2 * LANES), lambda bi, p, i: (bi, i, p)),
                  pl.BlockSpec((1, s, 2 * LANES), lambda bi, p, i: (bi, 0, p)),
                  pl.BlockSpec((1, s, LANES), lambda bi, p, i: (bi, 0, p)),
                  pl.BlockSpec((1, 2, LANES, LANES), lambda bi, p, i: (bi, p, 0, 0)),
                  pl.BlockSpec((2, 2, t, t), lambda bi, p, i: (0, p, 0, 0))],
        out_specs=pl.BlockSpec((1, t, LANES), lambda bi, p, i: (bi, i, p)),
        out_shape=jax.ShapeDtypeStruct((b, s, MOBA_COLS), BF16),
        scratch_shapes=[pltpu.VMEM((2, t, LANES), BF16),
                        pltpu.VMEM((2, t, 1), F32),
                        pltpu.VMEM((2, t, 1), F32),
                        pltpu.VMEM((2, t, LANES), F32)],
        compiler_params=_params(3),
        name="moba_attn",
    )(mq, mk, mv, km, bias)


ROUTER_GROUP_LANE = N_EXPERTS


def _mid_kernel(yd_ref, ym_ref, x_ref, wo_ref, pg_ref, g1_ref, fg_ref, sc2_ref, sh2_ref,
                wr_ref, br_ref, x1_ref, h2_ref, gates_ref):
    y = (jnp.dot(yd_ref[0], wo_ref[0:DIFF_COLS], preferred_element_type=F32)
         + jnp.dot(ym_ref[0], wo_ref[DIFF_COLS:D_MODEL], preferred_element_type=F32))
    x1 = x_ref[0] + g1_ref[0] * (_rms(y) * pg_ref[...])
    x1_ref[0] = x1
    h2 = _rms(x1) * fg_ref[...] * (1.0 + sc2_ref[0]) + sh2_ref[0]
    h2_ref[0] = h2.astype(BF16)

    logits = jnp.dot(h2, wr_ref[...], precision=HIGHEST, preferred_element_type=F32) + br_ref[...]
    lane = lax.broadcasted_iota(jnp.int32, logits.shape, 1)
    is_group = (lane >= ROUTER_GROUP_LANE) & (lane < ROUTER_GROUP_LANE + N_GROUPS)
    gl = jnp.where(is_group, logits, -jnp.inf)
    gmax = jnp.max(gl, axis=1, keepdims=True)
    g_idx = jnp.min(jnp.where(gl == gmax, lane - ROUTER_GROUP_LANE, N_GROUPS), axis=1, keepdims=True)
    g_w = 1.0 / jnp.sum(jnp.exp(gl - gmax), axis=1, keepdims=True)
    in_group = (lane < N_EXPERTS) & (lane // EXPERTS_PER_GROUP == g_idx)
    el = jnp.where(in_group, logits, -jnp.inf)
    m1 = jnp.max(el, axis=1, keepdims=True)
    i1 = jnp.min(jnp.where(el == m1, lane, LANES), axis=1, keepdims=True)
    el2 = jnp.where(lane == i1, -jnp.inf, el)
    m2 = jnp.max(el2, axis=1, keepdims=True)
    i2 = jnp.min(jnp.where(el2 == m2, lane, LANES), axis=1, keepdims=True)
    e2 = jnp.exp(m2 - m1)
    w1 = g_w / (1.0 + e2)
    w2 = g_w * e2 / (1.0 + e2)
    gates_ref[0] = jnp.where(lane == i1, w1, 0.0) + jnp.where(lane == i2, w2, 0.0)


def _mid(yd, ym, x, w_out, post_mix_gain, g1, pre_ffn_gain, sc2, sh2, w_router, b_router):
    b, s, d = x.shape
    t = MID_TILE
    tok = lambda width: pl.BlockSpec((1, t, width), lambda bi, i: (bi, i, 0))
    vec = pl.BlockSpec((1, 1, d), lambda bi, i: (bi, 0, 0))
    row = lambda width: pl.BlockSpec((1, width), lambda bi, i: (0, 0))
    return pl.pallas_call(
        _mid_kernel,
        grid=(b, s // t),
        in_specs=[tok(DIFF_COLS), tok(MOBA_COLS), tok(d),
                  pl.BlockSpec((d, d), lambda bi, i: (0, 0)),
                  row(d), vec, row(d), vec, vec,
                  pl.BlockSpec((d, LANES), lambda bi, i: (0, 0)), row(LANES)],
        out_specs=[tok(d), tok(d), tok(LANES)],
        out_shape=[jax.ShapeDtypeStruct((b, s, d), F32),
                   jax.ShapeDtypeStruct((b, s, d), BF16),
                   jax.ShapeDtypeStruct((b, s, LANES), F32)],
        compiler_params=_params(2),
        name="out_proj_router",
    )(yd, ym, x, w_out, post_mix_gain, g1, pre_ffn_gain, sc2, sh2, w_router, b_router)


def _moe_kernel(h_ref, gates_ref, x1_ref, g2_ref, pg_ref, wg_ref, wu_ref, wd_ref, o_ref, acc):
    e = pl.program_id(2)

    @pl.when(e == 0)
    def _():
        acc[...] = jnp.zeros(acc.shape, F32)

    h = h_ref[0]
    a = jnp.dot(h, wg_ref[0], preferred_element_type=F32)
    u = jnp.dot(h, wu_ref[0], preferred_element_type=F32)
    hid = (a * jax.nn.sigmoid(a)) * u
    out = jnp.dot(hid.astype(BF16), wd_ref[0], preferred_element_type=F32)
    pick = jnp.where(lax.broadcasted_iota(jnp.int32, (LANES, LANES), 0) == e, 1.0, 0.0)
    gcol = jnp.dot(gates_ref[0], pick, precision=HIGHEST, preferred_element_type=F32)
    acc[...] += jnp.concatenate([gcol] * (D_MODEL // LANES), axis=1) * out

    @pl.when(e == N_EXPERTS - 1)
    def _():
        o_ref[0] = x1_ref[0] + g2_ref[0] * (_rms(acc[...]) * pg_ref[...])


def _moe(h2, gates, x1, g2, post_ffn_gain, w_gate, w_up, w_down):
    b, s, d = x1.shape
    t = MOE_TILE
    tok = lambda width: pl.BlockSpec((1, t, width), lambda bi, i, e: (bi, i, 0))
    return pl.pallas_call(
        _moe_kernel,
        grid=(b, s // t, N_EXPERTS),
        in_specs=[tok(d), tok(LANES), tok(d),
                  pl.BlockSpec((1, 1, d), lambda bi, i, e: (bi, 0, 0)),
                  pl.BlockSpec((1, d), lambda bi, i, e: (0, 0)),
                  pl.BlockSpec((1, d, EXPERT_FF), lambda bi, i, e: (e, 0, 0)),
                  pl.BlockSpec((1, d, EXPERT_FF), lambda bi, i, e: (e, 0, 0)),
                  pl.BlockSpec((1, EXPERT_FF, d), lambda bi, i, e: (e, 0, 0))],
        out_specs=tok(d),
        out_shape=jax.ShapeDtypeStruct((b, s, d), F32),
        scratch_shapes=[pltpu.VMEM((t, d), F32)],
        compiler_params=_params(3),
        name="moe_experts",
    )(h2, gates, x1, g2, post_ffn_gain, w_gate, w_up, w_down)


def _augment_w_in(w_in):
    d = w_in.shape[0]
    scale = HEAD_DIM ** -0.5
    dq, dk, dv, mq, mk, mv = jnp.split(w_in, [512, 1024, 1536, 2048, 2560], axis=1)

    def pad_heads(w):
        w = w.reshape(d, N_MOBA_HEADS, HEAD_DIM)
        return jnp.pad(w, ((0, 0), (0, 0), (0, LANES - HEAD_DIM))).reshape(d, N_MOBA_HEADS * LANES)

    return jnp.concatenate([dq * scale, dk, dv, pad_heads(mq * scale), pad_heads(mk), mv],
                           axis=1).astype(BF16)


def kernel(x, c, w_ada, b_ada, pre_mix_gain, post_mix_gain, pre_ffn_gain, post_ffn_gain, w_in, lambda_q1, lambda_k1, lambda_q2, lambda_k2, diff_head_gain, w_out, rel_bias, w_group, b_group, w_expert, b_expert, w_gate, w_up, w_down):
    b, s, d = x.shape
    depth = w_in.shape[0]
    tab_t = rel_bias.T
    bias_diff = _bias_tiles(tab_t, DIFF_TILE, 0, N_DIFF_HEADS)
    bias_moba = _bias_tiles(tab_t, MOBA_BLOCK, N_DIFF_HEADS, N_MOBA_HEADS)
    c_pad = jnp.pad(c, ((0, 8 - b), (0, 0)))
    for l in range(depth):
        lambda_init = 0.8 - 0.6 * math.exp(-0.3 * l)
        mod = _ada(c_pad, w_ada[l], b_ada[l][None])[:b]
        sh1, sc1, g1, sh2, sc2, g2 = [m[:, None, :] for m in jnp.split(mod, 6, axis=-1)]

        dq, dk, dv, mq, mk, mv, kmean = _inproj(x, pre_mix_gain[l][None], sc1, sh1,
                                                _augment_w_in(w_in[l]))
        y_diff = _diff_attention(dq, dk, dv, bias_diff, lambda_q1[l][None], lambda_k1[l][None],
                                 lambda_q2[l][None], lambda_k2[l][None],
                                 diff_head_gain[l][None], lambda_init)
        nkb = s // MOBA_BLOCK
        km = kmean.reshape(b, nkb, N_MOBA_HEADS, LANES)[..., :HEAD_DIM]
        km = jnp.pad(km.transpose(0, 2, 3, 1),
                     ((0, 0), (0, 0), (0, LANES - HEAD_DIM), (HEAD_DIM, LANES - HEAD_DIM - nkb)))
        y_moba = _moba_attention(mq, mk, mv, km, bias_moba)

        w_router = jnp.pad(jnp.concatenate([w_expert[l], w_group[l]], axis=1),
                           ((0, 0), (0, LANES - N_EXPERTS - N_GROUPS)))
        b_router = jnp.pad(jnp.concatenate([b_expert[l], b_group[l]]),
                           (0, LANES - N_EXPERTS - N_GROUPS))[None]
        x1, h2, gates = _mid(y_diff, y_moba, x, w_out[l].astype(BF16), post_mix_gain[l][None], g1,
                             pre_ffn_gain[l][None], sc2, sh2, w_router, b_router)
        x = _moe(h2, gates, x1, g2, post_ffn_gain[l][None],
                 w_gate[l].astype(BF16), w_up[l].astype(BF16), w_down[l].astype(BF16))
    return x
```

```python
import functools
import math

import jax
import jax.numpy as jnp
from jax import lax
from jax.experimental import pallas as pl
from jax.experimental.pallas import tpu as pltpu

F32 = jnp.float32
BF16 = jnp.bfloat16
HIGHEST = lax.Precision.HIGHEST

D_MODEL = 1024
HEAD_DIM = 64
N_DIFF_HEADS = 4
N_MOBA_HEADS = 8
DIFF_COLS = 512
MOBA_COLS = 512
MOBA_BLOCK = 256
MOBA_TOPK = 3
N_BUCKETS = 32
MAX_EXACT = N_BUCKETS // 2
MAX_DISTANCE = 128
N_GROUPS = 4
EXPERTS_PER_GROUP = 8
N_EXPERTS = 32
EXPERT_FF = 512
EPS = 1e-6
NEG = -1e30
LANES = 128
LOG2E = math.log2(math.e)

IN_TILE = 512
ATTN_TILE = 512
MID_TILE = 512
MOE_TILE = 1024
VMEM_LIMIT = 56 * 1024 * 1024


def _params(n_axes, vmem=VMEM_LIMIT):
    return pltpu.CompilerParams(dimension_semantics=("arbitrary",) * n_axes,
                                vmem_limit_bytes=vmem)


def _rms(v):
    return v * lax.rsqrt(jnp.mean(v * v, axis=-1, keepdims=True) + EPS)


def _ada_kernel(c_ref, w_ref, b_ref, o_ref):
    c = c_ref[...]
    s = c * jax.nn.sigmoid(c)
    o_ref[...] = jnp.dot(s, w_ref[...], precision=HIGHEST,
                         preferred_element_type=F32) + b_ref[...]


def _ada(c_pad, w_ada, b_ada):
    n = w_ada.shape[1]
    return pl.pallas_call(
        _ada_kernel,
        grid=(n // D_MODEL,),
        in_specs=[pl.BlockSpec((8, D_MODEL), lambda j: (0, 0)),
                  pl.BlockSpec((D_MODEL, D_MODEL), lambda j: (0, j)),
                  pl.BlockSpec((1, D_MODEL), lambda j: (0, j))],
        out_specs=pl.BlockSpec((8, D_MODEL), lambda j: (0, j)),
        out_shape=jax.ShapeDtypeStruct((8, n), F32),
        compiler_params=_params(1),
        name="ada_mod",
    )(c_pad, w_ada, b_ada)


def _bias_kernel(tab_ref, o_ref, *, tile, head0):
    w = pl.program_id(0)
    h = pl.program_id(1) + head0
    r = lax.broadcasted_iota(jnp.int32, (tile, tile), 0)
    c = lax.broadcasted_iota(jnp.int32, (tile, tile), 1)
    d = r - c + w * tile
    n = jnp.maximum(d, 0)
    nf = jnp.maximum(n, 1).astype(F32)
    large = MAX_EXACT + (jnp.log(nf / MAX_EXACT) / math.log(MAX_DISTANCE / MAX_EXACT)
                         * (N_BUCKETS - MAX_EXACT)).astype(jnp.int32)
    large = jnp.minimum(large, N_BUCKETS - 1)
    bucket = jnp.where(n < MAX_EXACT, n, large)
    far = tab_ref[h, N_BUCKETS - 1]
    val = jnp.zeros((tile, tile), F32)
    for b in range(N_BUCKETS - 1):
        val = jnp.where(bucket == b, (tab_ref[h, b] - far) * LOG2E, val)
    o_ref[0, 0] = jnp.where(d >= 0, val, NEG)


def _bias_tiles(tab_t, tile, head0, n_heads):
    return pl.pallas_call(
        functools.partial(_bias_kernel, tile=tile, head0=head0),
        grid=(2, n_heads),
        in_specs=[pl.BlockSpec(memory_space=pltpu.SMEM)],
        out_specs=pl.BlockSpec((1, 1, tile, tile), lambda w, h: (w, h, 0, 0)),
        out_shape=jax.ShapeDtypeStruct((2, n_heads, tile, tile), F32),
        compiler_params=_params(2),
        name="bias_tiles",
    )(tab_t)


W_DQ, W_DK, W_DV = 0, 512, 1024
W_MQ, W_MK, W_MV = 1536, 2560, 3584
W_END = 4096


def _inproj_kernel(x_ref, gain_ref, sc_ref, sh_ref, w_ref,
                   dq_ref, dk_ref, dv_ref, mq_ref, mk_ref, mv_ref, km_ref):
    i = pl.program_id(1)
    h = _rms(x_ref[0]) * gain_ref[...]
    h = h * (1.0 + sc_ref[0]) + sh_ref[0]
    hb = h.astype(BF16)

    def proj(lo, hi):
        return jnp.dot(hb, w_ref[:, lo:hi], preferred_element_type=F32)

    dq_ref[0] = proj(W_DQ, W_DK).astype(BF16)
    dk_ref[0] = proj(W_DK, W_DV).astype(BF16)
    dv_ref[0] = proj(W_DV, W_MQ).astype(BF16)
    mq_ref[0] = proj(W_MQ, W_MK).astype(BF16)
    mv_ref[0] = proj(W_MV, W_END).astype(BF16)
    mk = proj(W_MK, W_MV)
    for r in range(IN_TILE // MOBA_BLOCK):
        km_ref[0, r] = jnp.mean(mk[r * MOBA_BLOCK:(r + 1) * MOBA_BLOCK], axis=0, keepdims=True)
    row = lax.broadcasted_iota(jnp.int32, mk.shape, 0)
    lane = lax.broadcasted_iota(jnp.int32, mk.shape, 1)
    blk = (i * IN_TILE + row) // MOBA_BLOCK
    onehot = (lane % LANES) - HEAD_DIM == blk
    mk_ref[0] = jnp.where(onehot, 1.0, mk).astype(BF16)


def _inproj(x, gain, sc1, sh1, w_aug):
    b, s, d = x.shape
    nkb = s // MOBA_BLOCK
    tok = lambda width: pl.BlockSpec((1, IN_TILE, width), lambda bi, i: (bi, i, 0))
    vec = pl.BlockSpec((1, 1, d), lambda bi, i: (bi, 0, 0))
    shp = lambda width: jax.ShapeDtypeStruct((b, s, width), BF16)
    return pl.pallas_call(
        _inproj_kernel,
        grid=(b, s // IN_TILE),
        in_specs=[tok(d), pl.BlockSpec((1, d), lambda bi, i: (0, 0)), vec, vec,
                  pl.BlockSpec((d, W_END), lambda bi, i: (0, 0))],
        out_specs=[tok(512), tok(512), tok(512), tok(1024), tok(1024), tok(512),
                   pl.BlockSpec((1, IN_TILE // MOBA_BLOCK, 1, 1024), lambda bi, i: (bi, i, 0, 0))],
        out_shape=[shp(512), shp(512), shp(512), shp(1024), shp(1024), shp(512),
                   jax.ShapeDtypeStruct((b, nkb, 1, 1024), F32)],
        compiler_params=_params(2),
        name="in_proj",
    )(x, gain, sc1, sh1, w_aug)


def _softmax_step(s, v, m_ref, l_ref, acc_ref, idx):
    groups = [s[:, g * LANES:(g + 1) * LANES] for g in range(s.shape[1] // LANES)]
    m_prev = m_ref[idx]
    m_tile = functools.reduce(jnp.maximum, groups)
    m_next = jnp.maximum(m_prev, jnp.max(m_tile, axis=1, keepdims=True))
    alpha = jnp.exp2(m_prev - m_next)
    ps = [jnp.exp2(g - m_next) for g in groups]
    l_ref[idx] = alpha * l_ref[idx] + functools.reduce(jnp.add, ps)
    p = jnp.concatenate(ps, axis=1).astype(BF16)
    acc_ref[idx] = alpha * acc_ref[idx] + jnp.dot(p, v, preferred_element_type=F32)
    m_ref[idx] = m_next


def _softmax_result(l_ref, acc_ref, idx):
    return acc_ref[idx] / jnp.sum(l_ref[idx], axis=1, keepdims=True)


def _nt_dot(a, b):
    return lax.dot_general(a, b, (((1,), (1,)), ((), ())), preferred_element_type=F32)


def _init_softmax_state(m_s, l_s, acc_s):
    m_s[...] = jnp.full(m_s.shape, -jnp.inf, F32)
    l_s[...] = jnp.zeros(l_s.shape, F32)
    acc_s[...] = jnp.zeros(acc_s.shape, F32)


def _causal_sweep(i, step):
    step(i, 0)

    @pl.when(i > 0)
    def _():
        step(i - 1, 1)

    n_far = jnp.maximum(i - 1, 0)

    def far_pair(jj, carry):
        step(2 * jj, None)
        step(2 * jj + 1, None)
        return carry

    lax.fori_loop(0, n_far // 2, far_pair, 0)

    @pl.when(n_far % 2 == 1)
    def _():
        step(n_far - 1, None)


def _attn_scratch():
    t = ATTN_TILE
    return [pltpu.VMEM((2, t, LANES), BF16), pltpu.VMEM((2, t, LANES), F32),
            pltpu.VMEM((2, t, LANES), F32), pltpu.VMEM((2, t, LANES), F32)]


def _diff_kernel(lq1_ref, lk1_ref, lq2_ref, lk2_ref, q_ref, k_ref, v_ref, bias_ref, gain_ref,
                 o_ref, qs, m_s, l_s, acc_s, *, lambda_init):
    i = pl.program_id(2)
    t = ATTN_TILE
    q = q_ref[0]
    lane = lax.broadcasted_iota(jnp.int32, q.shape, 1)
    zero = jnp.zeros_like(q)
    qs[0] = jnp.where(lane < HEAD_DIM, q, zero)
    qs[1] = jnp.where(lane >= HEAD_DIM, q, zero)
    _init_softmax_state(m_s, l_s, acc_s)

    def step(j, which):
        off = pl.multiple_of(j * t, t)
        k = k_ref[0, pl.ds(off, t), :]
        v = v_ref[0, pl.ds(off, t), :]
        for m in range(2):
            s = _nt_dot(qs[m], k)
            if which is not None:
                s = s + bias_ref[which, 0]
            _softmax_step(s, v, m_s, l_s, acc_s, m)

    _causal_sweep(i, step)

    lam = (jnp.exp(jnp.sum(lq1_ref[...] * lk1_ref[...], axis=-1, keepdims=True))
           - jnp.exp(jnp.sum(lq2_ref[...] * lk2_ref[...], axis=-1, keepdims=True)) + lambda_init)
    a = _softmax_result(l_s, acc_s, 0) - lam * _softmax_result(l_s, acc_s, 1)
    y = _rms(a) * gain_ref[...] * (1.0 - lambda_init)
    o_ref[0] = y.astype(BF16)


def _diff_attention(dq, dk, dv, bias, lq1, lk1, lq2, lk2, head_gain, lambda_init):
    b, s, _ = dq.shape
    t = ATTN_TILE
    lam_spec = pl.BlockSpec((1, HEAD_DIM), lambda bi, h, i: (0, 0))
    full = pl.BlockSpec((1, s, LANES), lambda bi, h, i: (bi, 0, h))
    return pl.pallas_call(
        functools.partial(_diff_kernel, lambda_init=lambda_init),
        grid=(b, N_DIFF_HEADS, s // t),
        in_specs=[lam_spec, lam_spec, lam_spec, lam_spec,
                  pl.BlockSpec((1, t, LANES), lambda bi, h, i: (bi, i, h)),
                  full, full,
                  pl.BlockSpec((2, 1, t, t), lambda bi, h, i: (0, h, 0, 0)),
                  pl.BlockSpec((1, LANES), lambda bi, h, i: (0, 0))],
        out_specs=pl.BlockSpec((1, t, LANES), lambda bi, h, i: (bi, i, h)),
        out_shape=jax.ShapeDtypeStruct((b, s, DIFF_COLS), BF16),
        scratch_shapes=_attn_scratch(),
        compiler_params=_params(3),
        name="diff_attn",
    )(lq1, lk1, lq2, lk2, dq, dk, dv, bias, head_gain)


def _moba_kernel(q_ref, k_ref, v_ref, km_ref, bias_ref, o_ref, qs, m_s, l_s, acc_s):
    i = pl.program_id(2)
    t = ATTN_TILE
    lane = lax.broadcasted_iota(jnp.int32, (t, LANES), 1)
    row = lax.broadcasted_iota(jnp.int32, (t, LANES), 0)
    blk = lane - HEAD_DIM
    own = i * (t // MOBA_BLOCK) + row // MOBA_BLOCK
    for hh in range(2):
        qa = q_ref[0, :, hh * LANES:(hh + 1) * LANES]
        g = jnp.dot(qa.astype(F32), km_ref[0, hh], precision=HIGHEST,
                    preferred_element_type=F32)
        g = jnp.where((blk >= 0) & (blk < own), g, -jnp.inf)
        sel = blk == own
        for _ in range(MOBA_TOPK):
            mx = jnp.max(g, axis=1, keepdims=True)
            first = jnp.min(jnp.where(g == mx, lane, 2 * LANES), axis=1, keepdims=True)
            pick = (lane == first) & (mx > -jnp.inf)
            sel = sel | pick
            g = jnp.where(pick, -jnp.inf, g)
        gate = jnp.where(sel, 0.0, NEG).astype(BF16)
        qs[hh] = jnp.where(lane < HEAD_DIM, qa, gate)
    _init_softmax_state(m_s, l_s, acc_s)

    def step(j, which):
        off = pl.multiple_of(j * t, t)
        k = k_ref[0, pl.ds(off, t), :]
        v = v_ref[0, pl.ds(off, t), :]
        for hh in range(2):
            s = _nt_dot(qs[hh], k[:, hh * LANES:(hh + 1) * LANES])
            if which is not None:
                s = s + bias_ref[which, hh]
            _softmax_step(s, v, m_s, l_s, acc_s, hh)

    _causal_sweep(i, step)

    o = jnp.where(lane < HEAD_DIM, _softmax_result(l_s, acc_s, 0), _softmax_result(l_s, acc_s, 1))
    o_ref[0] = o.astype(BF16)


def _moba_attention(mq, mk, mv, km, bias):
    b, s, _ = mq.shape
    t = ATTN_TILE
    return pl.pallas_call(
        _moba_kernel,
        grid=(b, N_MOBA_HEADS // 2, s // t),
        in_specs=[pl.BlockSpec((1, t, 2 * LANES), lambda bi, p, i: (bi, i, p)),
                  pl.BlockSpec((1, s, 2 * LANES), lambda bi, p, i: (bi, 0, p)),
                  pl.BlockSpec((1, s, LANES), lambda bi, p, i: (bi, 0, p)),
                  pl.BlockSpec((1, 2, LANES, LANES), lambda bi, p, i: (bi, p, 0, 0)),
                  pl.BlockSpec((2, 2, t, t), lambda bi, p, i: (0, p, 0, 0))],
        out_specs=pl.BlockSpec((1, t, LANES), lambda bi, p, i: (bi, i, p)),
        out_shape=jax.ShapeDtypeStruct((b, s, MOBA_COLS), BF16),
        scratch_shapes=_attn_scratch(),
        compiler_params=_params(3),
        name="moba_attn",
    )(mq, mk, mv, km, bias)


ROUTER_GROUP_LANE = N_EXPERTS


def _mid_kernel(yd_ref, ym_ref, x_ref, wo_ref, pg_ref, g1_ref, fg_ref, sc2_ref, sh2_ref,
                wr_ref, br_ref, x1_ref, h2_ref, gates_ref):
    y = (jnp.dot(yd_ref[0], wo_ref[0:DIFF_COLS], preferred_element_type=F32)
         + jnp.dot(ym_ref[0], wo_ref[DIFF_COLS:D_MODEL], preferred_element_type=F32))
    x1 = x_ref[0] + g1_ref[0] * (_rms(y) * pg_ref[...])
    x1_ref[0] = x1
    h2 = _rms(x1) * fg_ref[...] * (1.0 + sc2_ref[0]) + sh2_ref[0]
    h2_ref[0] = h2.astype(BF16)

    logits = jnp.dot(h2, wr_ref[...], precision=HIGHEST, preferred_element_type=F32) + br_ref[...]
    lane = lax.broadcasted_iota(jnp.int32, logits.shape, 1)
    is_group = (lane >= ROUTER_GROUP_LANE) & (lane < ROUTER_GROUP_LANE + N_GROUPS)
    gl = jnp.where(is_group, logits, -jnp.inf)
    gmax = jnp.max(gl, axis=1, keepdims=True)
    g_idx = jnp.min(jnp.where(gl == gmax, lane - ROUTER_GROUP_LANE, N_GROUPS), axis=1, keepdims=True)
    g_w = 1.0 / jnp.sum(jnp.exp(gl - gmax), axis=1, keepdims=True)
    in_group = (lane < N_EXPERTS) & (lane // EXPERTS_PER_GROUP == g_idx)
    el = jnp.where(in_group, logits, -jnp.inf)
    m1 = jnp.max(el, axis=1, keepdims=True)
    i1 = jnp.min(jnp.where(el == m1, lane, LANES), axis=1, keepdims=True)
    el2 = jnp.where(lane == i1, -jnp.inf, el)
    m2 = jnp.max(el2, axis=1, keepdims=True)
    i2 = jnp.min(jnp.where(el2 == m2, lane, LANES), axis=1, keepdims=True)
    e2 = jnp.exp(m2 - m1)
    w1 = g_w / (1.0 + e2)
    w2 = g_w * e2 / (1.0 + e2)
    gates_ref[0] = jnp.where(lane == i1, w1, 0.0) + jnp.where(lane == i2, w2, 0.0)


def _mid(yd, ym, x, w_out, post_mix_gain, g1, pre_ffn_gain, sc2, sh2, w_router, b_router):
    b, s, d = x.shape
    t = MID_TILE
    tok = lambda width: pl.BlockSpec((1, t, width), lambda bi, i: (bi, i, 0))
    vec = pl.BlockSpec((1, 1, d), lambda bi, i: (bi, 0, 0))
    row = lambda width: pl.BlockSpec((1, width), lambda bi, i: (0, 0))
    return pl.pallas_call(
        _mid_kernel,
        grid=(b, s // t),
        in_specs=[tok(DIFF_COLS), tok(MOBA_COLS), tok(d),
                  pl.BlockSpec((d, d), lambda bi, i: (0, 0)),
                  row(d), vec, row(d), vec, vec,
                  pl.BlockSpec((d, LANES), lambda bi, i: (0, 0)), row(LANES)],
        out_specs=[tok(d), tok(d), tok(LANES)],
        out_shape=[jax.ShapeDtypeStruct((b, s, d), F32),
                   jax.ShapeDtypeStruct((b, s, d), BF16),
                   jax.ShapeDtypeStruct((b, s, LANES), F32)],
        compiler_params=_params(2),
        name="out_proj_router",
    )(yd, ym, x, w_out, post_mix_gain, g1, pre_ffn_gain, sc2, sh2, w_router, b_router)


def _moe_kernel(h_ref, gates_ref, x1_ref, g2_ref, pg_ref, wg_ref, wu_ref, wd_ref, o_ref, acc):
    e = pl.program_id(2)

    @pl.when(e == 0)
    def _():
        acc[...] = jnp.zeros(acc.shape, F32)

    h = h_ref[0]
    a = jnp.dot(h, wg_ref[0], preferred_element_type=F32)
    u = jnp.dot(h, wu_ref[0], preferred_element_type=F32)
    hid = (a * jax.nn.sigmoid(a)) * u
    out = jnp.dot(hid.astype(BF16), wd_ref[0], preferred_element_type=F32)
    pick = jnp.where(lax.broadcasted_iota(jnp.int32, (LANES, LANES), 0) == e, 1.0, 0.0)
    gcol = jnp.dot(gates_ref[0], pick, precision=HIGHEST, preferred_element_type=F32)
    acc[...] += jnp.concatenate([gcol] * (D_MODEL // LANES), axis=1) * out

    @pl.when(e == N_EXPERTS - 1)
    def _():
        o_ref[0] = x1_ref[0] + g2_ref[0] * (_rms(acc[...]) * pg_ref[...])


def _moe(h2, gates, x1, g2, post_ffn_gain, w_gate, w_up, w_down):
    b, s, d = x1.shape
    t = MOE_TILE
    tok = lambda width: pl.BlockSpec((1, t, width), lambda bi, i, e: (bi, i, 0))
    return pl.pallas_call(
        _moe_kernel,
        grid=(b, s // t, N_EXPERTS),
        in_specs=[tok(d), tok(LANES), tok(d),
                  pl.BlockSpec((1, 1, d), lambda bi, i, e: (bi, 0, 0)),
                  pl.BlockSpec((1, d), lambda bi, i, e: (0, 0)),
                  pl.BlockSpec((1, d, EXPERT_FF), lambda bi, i, e: (e, 0, 0)),
                  pl.BlockSpec((1, d, EXPERT_FF), lambda bi, i, e: (e, 0, 0)),
                  pl.BlockSpec((1, EXPERT_FF, d), lambda bi, i, e: (e, 0, 0))],
        out_specs=tok(d),
        out_shape=jax.ShapeDtypeStruct((b, s, d), F32),
        scratch_shapes=[pltpu.VMEM((t, d), F32)],
        compiler_params=_params(3),
        name="moe_experts",
    )(h2, gates, x1, g2, post_ffn_gain, w_gate, w_up, w_down)


def _augment_w_in(w_in):
    d = w_in.shape[0]
    scale = HEAD_DIM ** -0.5 * LOG2E
    dq, dk, dv, mq, mk, mv = jnp.split(w_in, [512, 1024, 1536, 2048, 2560], axis=1)

    def pad_heads(w):
        w = w.reshape(d, N_MOBA_HEADS, HEAD_DIM)
        return jnp.pad(w, ((0, 0), (0, 0), (0, LANES - HEAD_DIM))).reshape(d, N_MOBA_HEADS * LANES)

    return jnp.concatenate([dq * scale, dk, dv, pad_heads(mq * scale), pad_heads(mk), mv],
                           axis=1).astype(BF16)


def kernel(x, c, w_ada, b_ada, pre_mix_gain, post_mix_gain, pre_ffn_gain, post_ffn_gain, w_in, lambda_q1, lambda_k1, lambda_q2, lambda_k2, diff_head_gain, w_out, rel_bias, w_group, b_group, w_expert, b_expert, w_gate, w_up, w_down):
    b, s, d = x.shape
    depth = w_in.shape[0]
    tab_t = rel_bias.T
    bias_diff = _bias_tiles(tab_t, ATTN_TILE, 0, N_DIFF_HEADS)
    bias_moba = _bias_tiles(tab_t, ATTN_TILE, N_DIFF_HEADS, N_MOBA_HEADS)
    c_pad = jnp.pad(c, ((0, 8 - b), (0, 0)))
    for l in range(depth):
        lambda_init = 0.8 - 0.6 * math.exp(-0.3 * l)
        mod = _ada(c_pad, w_ada[l], b_ada[l][None])[:b]
        sh1, sc1, g1, sh2, sc2, g2 = [m[:, None, :] for m in jnp.split(mod, 6, axis=-1)]

        dq, dk, dv, mq, mk, mv, kmean = _inproj(x, pre_mix_gain[l][None], sc1, sh1,
                                                _augment_w_in(w_in[l]))
        y_diff = _diff_attention(dq, dk, dv, bias_diff, lambda_q1[l][None], lambda_k1[l][None],
                                 lambda_q2[l][None], lambda_k2[l][None],
                                 diff_head_gain[l][None], lambda_init)
        nkb = s // MOBA_BLOCK
        km = kmean.reshape(b, nkb, N_MOBA_HEADS, LANES)[..., :HEAD_DIM]
        km = jnp.pad(km.transpose(0, 2, 3, 1),
                     ((0, 0), (0, 0), (0, LANES - HEAD_DIM), (HEAD_DIM, LANES - HEAD_DIM - nkb)))
        y_moba = _moba_attention(mq, mk, mv, km, bias_moba)

        w_router = jnp.pad(jnp.concatenate([w_expert[l], w_group[l]], axis=1),
                           ((0, 0), (0, LANES - N_EXPERTS - N_GROUPS)))
        b_router = jnp.pad(jnp.concatenate([b_expert[l], b_group[l]]),
                           (0, LANES - N_EXPERTS - N_GROUPS))[None]
        x1, h2, gates = _mid(y_diff, y_moba, x, w_out[l].astype(BF16), post_mix_gain[l][None], g1,
                             pre_ffn_gain[l][None], sc2, sh2, w_router, b_router)
        x = _moe(h2, gates, x1, g2, post_ffn_gain[l][None],
                 w_gate[l].astype(BF16), w_up[l].astype(BF16), w_down[l].astype(BF16))
    return x
```

```python
import functools
import math

import jax
import jax.numpy as jnp
from jax import lax
from jax.experimental import pallas as pl
from jax.experimental.pallas import tpu as pltpu

F32 = jnp.float32
BF16 = jnp.bfloat16
HIGHEST = lax.Precision.HIGHEST

D_MODEL = 1024
HEAD_DIM = 64
N_DIFF_HEADS = 4
N_MOBA_HEADS = 8
DIFF_COLS = 512
MOBA_COLS = 512
MOBA_BLOCK = 256
MOBA_TOPK = 3
N_BUCKETS = 32
MAX_EXACT = N_BUCKETS // 2
MAX_DISTANCE = 128
N_GROUPS = 4
EXPERTS_PER_GROUP = 8
N_EXPERTS = 32
EXPERT_FF = 512
EPS = 1e-6
NEG = -1e30
LANES = 128
LOG2E = math.log2(math.e)

IN_TILE = 512
ATTN_TILE = 512
MID_TILE = 512
MOE_TILE = 1024
EXPERT_CAP = 128
EXPERT_TILE = 512
VMEM_LIMIT = 56 * 1024 * 1024


def _params(n_axes, vmem=VMEM_LIMIT):
    return pltpu.CompilerParams(dimension_semantics=("arbitrary",) * n_axes,
                                vmem_limit_bytes=vmem)


def _rms(v):
    return v * lax.rsqrt(jnp.mean(v * v, axis=-1, keepdims=True) + EPS)


def _ada_kernel(c_ref, w_ref, b_ref, o_ref):
    c = c_ref[...]
    s = c * jax.nn.sigmoid(c)
    o_ref[...] = jnp.dot(s, w_ref[...], precision=HIGHEST,
                         preferred_element_type=F32) + b_ref[...]


def _ada(c_pad, w_ada, b_ada):
    n = w_ada.shape[1]
    return pl.pallas_call(
        _ada_kernel,
        grid=(n // D_MODEL,),
        in_specs=[pl.BlockSpec((8, D_MODEL), lambda j: (0, 0)),
                  pl.BlockSpec((D_MODEL, D_MODEL), lambda j: (0, j)),
                  pl.BlockSpec((1, D_MODEL), lambda j: (0, j))],
        out_specs=pl.BlockSpec((8, D_MODEL), lambda j: (0, j)),
        out_shape=jax.ShapeDtypeStruct((8, n), F32),
        compiler_params=_params(1),
        name="ada_mod",
    )(c_pad, w_ada, b_ada)


def _bias_kernel(tab_ref, o_ref, *, tile, head0):
    w = pl.program_id(0)
    h = pl.program_id(1) + head0
    r = lax.broadcasted_iota(jnp.int32, (tile, tile), 0)
    c = lax.broadcasted_iota(jnp.int32, (tile, tile), 1)
    d = r - c + w * tile
    n = jnp.maximum(d, 0)
    nf = jnp.maximum(n, 1).astype(F32)
    large = MAX_EXACT + (jnp.log(nf / MAX_EXACT) / math.log(MAX_DISTANCE / MAX_EXACT)
                         * (N_BUCKETS - MAX_EXACT)).astype(jnp.int32)
    large = jnp.minimum(large, N_BUCKETS - 1)
    bucket = jnp.where(n < MAX_EXACT, n, large)
    far = tab_ref[h, N_BUCKETS - 1]
    val = jnp.zeros((tile, tile), F32)
    for b in range(N_BUCKETS - 1):
        val = jnp.where(bucket == b, (tab_ref[h, b] - far) * LOG2E, val)
    o_ref[0, 0] = jnp.where(d >= 0, val, NEG)


def _bias_tiles(tab_t, tile, head0, n_heads):
    return pl.pallas_call(
        functools.partial(_bias_kernel, tile=tile, head0=head0),
        grid=(2, n_heads),
        in_specs=[pl.BlockSpec(memory_space=pltpu.SMEM)],
        out_specs=pl.BlockSpec((1, 1, tile, tile), lambda w, h: (w, h, 0, 0)),
        out_shape=jax.ShapeDtypeStruct((2, n_heads, tile, tile), F32),
        compiler_params=_params(2),
        name="bias_tiles",
    )(tab_t)


W_DQ, W_DK, W_DV = 0, 512, 1024
W_MQ, W_MK, W_MV = 1536, 2560, 3584
W_END = 4096


def _inproj_kernel(x_ref, gain_ref, sc_ref, sh_ref, w_ref,
                   dq_ref, dk_ref, dv_ref, mq_ref, mk_ref, mv_ref, km_ref):
    i = pl.program_id(1)
    h = _rms(x_ref[0]) * gain_ref[...]
    h = h * (1.0 + sc_ref[0]) + sh_ref[0]
    hb = h.astype(BF16)

    def proj(lo, hi):
        return jnp.dot(hb, w_ref[:, lo:hi], preferred_element_type=F32)

    dq_ref[0] = proj(W_DQ, W_DK).astype(BF16)
    dk_ref[0] = proj(W_DK, W_DV).astype(BF16)
    dv_ref[0] = proj(W_DV, W_MQ).astype(BF16)
    mq_ref[0] = proj(W_MQ, W_MK).astype(BF16)
    mv_ref[0] = proj(W_MV, W_END).astype(BF16)
    mk = proj(W_MK, W_MV)
    for r in range(IN_TILE // MOBA_BLOCK):
        km_ref[0, r] = jnp.mean(mk[r * MOBA_BLOCK:(r + 1) * MOBA_BLOCK], axis=0, keepdims=True)
    row = lax.broadcasted_iota(jnp.int32, mk.shape, 0)
    lane = lax.broadcasted_iota(jnp.int32, mk.shape, 1)
    blk = (i * IN_TILE + row) // MOBA_BLOCK
    onehot = (lane % LANES) - HEAD_DIM == blk
    mk_ref[0] = jnp.where(onehot, 1.0, mk).astype(BF16)


def _inproj(x, gain, sc1, sh1, w_aug):
    b, s, d = x.shape
    nkb = s // MOBA_BLOCK
    tok = lambda width: pl.BlockSpec((1, IN_TILE, width), lambda bi, i: (bi, i, 0))
    vec = pl.BlockSpec((1, 1, d), lambda bi, i: (bi, 0, 0))
    shp = lambda width: jax.ShapeDtypeStruct((b, s, width), BF16)
    return pl.pallas_call(
        _inproj_kernel,
        grid=(b, s // IN_TILE),
        in_specs=[tok(d), pl.BlockSpec((1, d), lambda bi, i: (0, 0)), vec, vec,
                  pl.BlockSpec((d, W_END), lambda bi, i: (0, 0))],
        out_specs=[tok(512), tok(512), tok(512), tok(1024), tok(1024), tok(512),
                   pl.BlockSpec((1, IN_TILE // MOBA_BLOCK, 1, 1024), lambda bi, i: (bi, i, 0, 0))],
        out_shape=[shp(512), shp(512), shp(512), shp(1024), shp(1024), shp(512),
                   jax.ShapeDtypeStruct((b, nkb, 1, 1024), F32)],
        compiler_params=_params(2),
        name="in_proj",
    )(x, gain, sc1, sh1, w_aug)


def _softmax_step(s, v, m_ref, l_ref, acc_ref, idx):
    groups = [s[:, g * LANES:(g + 1) * LANES] for g in range(s.shape[1] // LANES)]
    m_prev = m_ref[idx]
    m_tile = functools.reduce(jnp.maximum, groups)
    m_next = jnp.maximum(m_prev, jnp.max(m_tile, axis=1, keepdims=True))
    alpha = jnp.exp2(m_prev - m_next)
    ps = [jnp.exp2(g - m_next) for g in groups]
    l_ref[idx] = alpha * l_ref[idx] + functools.reduce(jnp.add, ps)
    p = jnp.concatenate(ps, axis=1).astype(BF16)
    acc_ref[idx] = alpha * acc_ref[idx] + jnp.dot(p, v, preferred_element_type=F32)
    m_ref[idx] = m_next


def _softmax_result(l_ref, acc_ref, idx):
    return acc_ref[idx] / jnp.sum(l_ref[idx], axis=1, keepdims=True)


def _nt_dot(a, b):
    return lax.dot_general(a, b, (((1,), (1,)), ((), ())), preferred_element_type=F32)


def _init_softmax_state(m_s, l_s, acc_s):
    m_s[...] = jnp.full(m_s.shape, -jnp.inf, F32)
    l_s[...] = jnp.zeros(l_s.shape, F32)
    acc_s[...] = jnp.zeros(acc_s.shape, F32)


def _causal_sweep(i, step):
    step(i, 0)

    @pl.when(i > 0)
    def _():
        step(i - 1, 1)

    n_far = jnp.maximum(i - 1, 0)

    def far_pair(jj, carry):
        step(2 * jj, None)
        step(2 * jj + 1, None)
        return carry

    lax.fori_loop(0, n_far // 2, far_pair, 0)

    @pl.when(n_far % 2 == 1)
    def _():
        step(n_far - 1, None)


def _attn_scratch():
    t = ATTN_TILE
    return [pltpu.VMEM((2, t, LANES), BF16), pltpu.VMEM((2, t, LANES), F32),
            pltpu.VMEM((2, t, LANES), F32), pltpu.VMEM((2, t, LANES), F32)]


def _diff_kernel(lq1_ref, lk1_ref, lq2_ref, lk2_ref, q_ref, k_ref, v_ref, bias_ref, gain_ref,
                 o_ref, qs, m_s, l_s, acc_s, *, lambda_init):
    i = pl.program_id(2)
    t = ATTN_TILE
    q = q_ref[0]
    lane = lax.broadcasted_iota(jnp.int32, q.shape, 1)
    zero = jnp.zeros_like(q)
    qs[0] = jnp.where(lane < HEAD_DIM, q, zero)
    qs[1] = jnp.where(lane >= HEAD_DIM, q, zero)
    _init_softmax_state(m_s, l_s, acc_s)

    def step(j, which):
        off = pl.multiple_of(j * t, t)
        k = k_ref[0, pl.ds(off, t), :]
        v = v_ref[0, pl.ds(off, t), :]
        for m in range(2):
            s = _nt_dot(qs[m], k)
            if which is not None:
                s = s + bias_ref[which, 0]
            _softmax_step(s, v, m_s, l_s, acc_s, m)

    _causal_sweep(i, step)

    lam = (jnp.exp(jnp.sum(lq1_ref[...] * lk1_ref[...], axis=-1, keepdims=True))
           - jnp.exp(jnp.sum(lq2_ref[...] * lk2_ref[...], axis=-1, keepdims=True)) + lambda_init)
    a = _softmax_result(l_s, acc_s, 0) - lam * _softmax_result(l_s, acc_s, 1)
    y = _rms(a) * gain_ref[...] * (1.0 - lambda_init)
    o_ref[0] = y.astype(BF16)


def _diff_attention(dq, dk, dv, bias, lq1, lk1, lq2, lk2, head_gain, lambda_init):
    b, s, _ = dq.shape
    t = ATTN_TILE
    lam_spec = pl.BlockSpec((1, HEAD_DIM), lambda bi, h, i: (0, 0))
    full = pl.BlockSpec((1, s, LANES), lambda bi, h, i: (bi, 0, h))
    return pl.pallas_call(
        functools.partial(_diff_kernel, lambda_init=lambda_init),
        grid=(b, N_DIFF_HEADS, s // t),
        in_specs=[lam_spec, lam_spec, lam_spec, lam_spec,
                  pl.BlockSpec((1, t, LANES), lambda bi, h, i: (bi, i, h)),
                  full, full,
                  pl.BlockSpec((2, 1, t, t), lambda bi, h, i: (0, h, 0, 0)),
                  pl.BlockSpec((1, LANES), lambda bi, h, i: (0, 0))],
        out_specs=pl.BlockSpec((1, t, LANES), lambda bi, h, i: (bi, i, h)),
        out_shape=jax.ShapeDtypeStruct((b, s, DIFF_COLS), BF16),
        scratch_shapes=_attn_scratch(),
        compiler_params=_params(3),
        name="diff_attn",
    )(lq1, lk1, lq2, lk2, dq, dk, dv, bias, head_gain)


def _moba_kernel(q_ref, k_ref, v_ref, km_ref, bias_ref, o_ref, qs, m_s, l_s, acc_s):
    i = pl.program_id(2)
    t = ATTN_TILE
    lane = lax.broadcasted_iota(jnp.int32, (t, LANES), 1)
    row = lax.broadcasted_iota(jnp.int32, (t, LANES), 0)
    blk = lane - HEAD_DIM
    own = i * (t // MOBA_BLOCK) + row // MOBA_BLOCK
    for hh in range(2):
        qa = q_ref[0, :, hh * LANES:(hh + 1) * LANES]
        g = jnp.dot(qa.astype(F32), km_ref[0, hh], precision=HIGHEST,
                    preferred_element_type=F32)
        g = jnp.where((blk >= 0) & (blk < own), g, -jnp.inf)
        sel = blk == own
        for _ in range(MOBA_TOPK):
            mx = jnp.max(g, axis=1, keepdims=True)
            first = jnp.min(jnp.where(g == mx, lane, 2 * LANES), axis=1, keepdims=True)
            pick = (lane == first) & (mx > -jnp.inf)
            sel = sel | pick
            g = jnp.where(pick, -jnp.inf, g)
        gate = jnp.where(sel, 0.0, NEG).astype(BF16)
        qs[hh] = jnp.where(lane < HEAD_DIM, qa, gate)
    _init_softmax_state(m_s, l_s, acc_s)

    def step(j, which):
        off = pl.multiple_of(j * t, t)
        k = k_ref[0, pl.ds(off, t), :]
        v = v_ref[0, pl.ds(off, t), :]
        for hh in range(2):
            s = _nt_dot(qs[hh], k[:, hh * LANES:(hh + 1) * LANES])
            if which is not None:
                s = s + bias_ref[which, hh]
            _softmax_step(s, v, m_s, l_s, acc_s, hh)

    _causal_sweep(i, step)

    o = jnp.where(lane < HEAD_DIM, _softmax_result(l_s, acc_s, 0), _softmax_result(l_s, acc_s, 1))
    o_ref[0] = o.astype(BF16)


def _moba_attention(mq, mk, mv, km, bias):
    b, s, _ = mq.shape
    t = ATTN_TILE
    return pl.pallas_call(
        _moba_kernel,
        grid=(b, N_MOBA_HEADS // 2, s // t),
        in_specs=[pl.BlockSpec((1, t, 2 * LANES), lambda bi, p, i: (bi, i, p)),
                  pl.BlockSpec((1, s, 2 * LANES), lambda bi, p, i: (bi, 0, p)),
                  pl.BlockSpec((1, s, LANES), lambda bi, p, i: (bi, 0, p)),
                  pl.BlockSpec((1, 2, LANES, LANES), lambda bi, p, i: (bi, p, 0, 0)),
                  pl.BlockSpec((2, 2, t, t), lambda bi, p, i: (0, p, 0, 0))],
        out_specs=pl.BlockSpec((1, t, LANES), lambda bi, p, i: (bi, i, p)),
        out_shape=jax.ShapeDtypeStruct((b, s, MOBA_COLS), BF16),
        scratch_shapes=_attn_scratch(),
        compiler_params=_params(3),
        name="moba_attn",
    )(mq, mk, mv, km, bias)


ROUTER_GROUP_LANE = N_EXPERTS


def _mid_kernel(yd_ref, ym_ref, x_ref, wo_ref, pg_ref, g1_ref, fg_ref, sc2_ref, sh2_ref,
                wr_ref, br_ref, x1_ref, h2_ref, gates_ref, cnt_ref, pick_ref, gw_ref, xs_ref):
    y = (jnp.dot(yd_ref[0], wo_ref[0:DIFF_COLS], preferred_element_type=F32)
         + jnp.dot(ym_ref[0], wo_ref[DIFF_COLS:D_MODEL], preferred_element_type=F32))
    x1 = x_ref[0] + g1_ref[0] * (_rms(y) * pg_ref[...])
    x1_ref[0] = x1
    h2 = _rms(x1) * fg_ref[...] * (1.0 + sc2_ref[0]) + sh2_ref[0]
    h2_ref[0] = h2.astype(BF16)

    logits = jnp.dot(h2, wr_ref[...], precision=HIGHEST, preferred_element_type=F32) + br_ref[...]
    lane = lax.broadcasted_iota(jnp.int32, logits.shape, 1)
    is_group = (lane >= ROUTER_GROUP_LANE) & (lane < ROUTER_GROUP_LANE + N_GROUPS)
    gl = jnp.where(is_group, logits, -jnp.inf)
    gmax = jnp.max(gl, axis=1, keepdims=True)
    g_idx = jnp.min(jnp.where(gl == gmax, lane - ROUTER_GROUP_LANE, N_GROUPS), axis=1, keepdims=True)
    g_w = 1.0 / jnp.sum(jnp.exp(gl - gmax), axis=1, keepdims=True)
    in_group = (lane < N_EXPERTS) & (lane // EXPERTS_PER_GROUP == g_idx)
    el = jnp.where(in_group, logits, -jnp.inf)
    m1 = jnp.max(el, axis=1, keepdims=True)
    i1 = jnp.min(jnp.where(el == m1, lane, LANES), axis=1, keepdims=True)
    el2 = jnp.where(lane == i1, -jnp.inf, el)
    m2 = jnp.max(el2, axis=1, keepdims=True)
    i2 = jnp.min(jnp.where(el2 == m2, lane, LANES), axis=1, keepdims=True)
    e2 = jnp.exp(m2 - m1)
    w1 = g_w / (1.0 + e2)
    w2 = g_w * e2 / (1.0 + e2)
    gates = jnp.where(lane == i1, w1, 0.0) + jnp.where(lane == i2, w2, 0.0)
    gates_ref[0] = gates

    t = MID_TILE
    used = jnp.where((lane == i1) | (lane == i2), 1.0, 0.0)
    cnt_ref[0] = jnp.sum(used, axis=0, keepdims=True)
    used_t = used.T
    gates_t = gates.T
    earlier = jnp.where(lax.broadcasted_iota(jnp.int32, (t, t), 0)
                        < lax.broadcasted_iota(jnp.int32, (t, t), 1), 1.0, 0.0).astype(BF16)
    rank_t = jnp.dot(used_t.astype(BF16), earlier, preferred_element_type=F32)
    slot_t = jnp.where(used_t > 0.0, rank_t, -1.0)
    r_iota = lax.broadcasted_iota(jnp.int32, (EXPERT_CAP, t), 0).astype(F32)
    h2b = h2.astype(BF16)
    for e0 in range(0, N_EXPERTS, EXPERTS_PER_GROUP):
        picks = []
        for e in range(e0, e0 + EXPERTS_PER_GROUP):
            hit = r_iota == slot_t[e:e + 1, :]
            picks.append(jnp.where(hit, 1.0, 0.0).astype(BF16))
            gw_ref[e] = jnp.sum(jnp.where(hit, gates_t[e:e + 1, :], 0.0), axis=1, keepdims=True)
        pick = jnp.concatenate(picks, axis=0)
        pick_ref[e0 * EXPERT_CAP:(e0 + EXPERTS_PER_GROUP) * EXPERT_CAP, :] = pick
        xs = jnp.dot(pick, h2b, preferred_element_type=F32)
        xs_ref[e0:e0 + EXPERTS_PER_GROUP] = xs.astype(BF16).reshape(
            EXPERTS_PER_GROUP, EXPERT_CAP, D_MODEL)


def _mid(yd, ym, x, w_out, post_mix_gain, g1, pre_ffn_gain, sc2, sh2, w_router, b_router):
    b, s, d = x.shape
    t = MID_TILE
    nt = s // t
    rows = b * nt * EXPERT_CAP
    tok = lambda width: pl.BlockSpec((1, t, width), lambda bi, i: (bi, i, 0))
    vec = pl.BlockSpec((1, 1, d), lambda bi, i: (bi, 0, 0))
    row = lambda width: pl.BlockSpec((1, width), lambda bi, i: (0, 0))
    per_expert = lambda width: pl.BlockSpec((N_EXPERTS, EXPERT_CAP, width),
                                            lambda bi, i: (0, bi * nt + i, 0))
    return pl.pallas_call(
        _mid_kernel,
        grid=(b, nt),
        in_specs=[tok(DIFF_COLS), tok(MOBA_COLS), tok(d),
                  pl.BlockSpec((d, d), lambda bi, i: (0, 0)),
                  row(d), vec, row(d), vec, vec,
                  pl.BlockSpec((d, LANES), lambda bi, i: (0, 0)), row(LANES)],
        out_specs=[tok(d), tok(d), tok(LANES),
                   pl.BlockSpec((1, 1, LANES), lambda bi, i: (bi * nt + i, 0, 0)),
                   pl.BlockSpec((N_EXPERTS * EXPERT_CAP, t), lambda bi, i: (bi * nt + i, 0)),
                   per_expert(1), per_expert(d)],
        out_shape=[jax.ShapeDtypeStruct((b, s, d), F32),
                   jax.ShapeDtypeStruct((b, s, d), BF16),
                   jax.ShapeDtypeStruct((b, s, LANES), F32),
                   jax.ShapeDtypeStruct((b * nt, 1, LANES), F32),
                   jax.ShapeDtypeStruct((b * nt * N_EXPERTS * EXPERT_CAP, t), BF16),
                   jax.ShapeDtypeStruct((N_EXPERTS, rows, 1), F32),
                   jax.ShapeDtypeStruct((N_EXPERTS, rows, d), BF16)],
        compiler_params=_params(2),
        name="out_proj_router",
    )(yd, ym, x, w_out, post_mix_gain, g1, pre_ffn_gain, sc2, sh2, w_router, b_router)


def _expert_kernel(x_ref, gw_ref, wg_ref, wu_ref, wd_ref, y_ref):
    x = x_ref[0]
    a = jnp.dot(x, wg_ref[0], preferred_element_type=F32)
    u = jnp.dot(x, wu_ref[0], preferred_element_type=F32)
    hid = (a * jax.nn.sigmoid(a)) * u
    y = jnp.dot(hid.astype(BF16), wd_ref[0], preferred_element_type=F32)
    y_ref[0] = (gw_ref[0] * y).astype(BF16)


def _experts(xs, gw, w_gate, w_up, w_down):
    n_e, rows, d = xs.shape
    t = EXPERT_TILE
    return pl.pallas_call(
        _expert_kernel,
        grid=(n_e, rows // t),
        in_specs=[pl.BlockSpec((1, t, d), lambda e, j: (e, j, 0)),
                  pl.BlockSpec((1, t, 1), lambda e, j: (e, j, 0)),
                  pl.BlockSpec((1, d, EXPERT_FF), lambda e, j: (e, 0, 0)),
                  pl.BlockSpec((1, d, EXPERT_FF), lambda e, j: (e, 0, 0)),
                  pl.BlockSpec((1, EXPERT_FF, d), lambda e, j: (e, 0, 0))],
        out_specs=pl.BlockSpec((1, t, d), lambda e, j: (e, j, 0)),
        out_shape=jax.ShapeDtypeStruct((n_e, rows, d), BF16),
        compiler_params=_params(2),
        name="routed_experts",
    )(xs, gw, w_gate, w_up, w_down)


def _combine_kernel(pick_ref, y_ref, x1_ref, g2_ref, pg_ref, o_ref):
    ys = y_ref[...].reshape(N_EXPERTS * EXPERT_CAP, D_MODEL)
    out = lax.dot_general(pick_ref[...], ys, (((0,), (0,)), ((), ())), preferred_element_type=F32)
    o_ref[0] = x1_ref[0] + g2_ref[0] * (_rms(out) * pg_ref[...])


def _combine(pick, ys, x1, g2, post_ffn_gain):
    b, s, d = x1.shape
    t = MID_TILE
    nt = s // t
    tok = pl.BlockSpec((1, t, d), lambda bi, i: (bi, i, 0))
    return pl.pallas_call(
        _combine_kernel,
        grid=(b, nt),
        in_specs=[pl.BlockSpec((N_EXPERTS * EXPERT_CAP, t), lambda bi, i: (bi * nt + i, 0)),
                  pl.BlockSpec((N_EXPERTS, EXPERT_CAP, d), lambda bi, i: (0, bi * nt + i, 0)),
                  tok,
                  pl.BlockSpec((1, 1, d), lambda bi, i: (bi, 0, 0)),
                  pl.BlockSpec((1, d), lambda bi, i: (0, 0))],
        out_specs=tok,
        out_shape=jax.ShapeDtypeStruct((b, s, d), F32),
        compiler_params=_params(2),
        name="moe_combine",
    )(pick, ys, x1, g2, post_ffn_gain)


def _moe_kernel(h_ref, gates_ref, x1_ref, g2_ref, pg_ref, wg_ref, wu_ref, wd_ref, o_ref, acc):
    e = pl.program_id(2)

    @pl.when(e == 0)
    def _():
        acc[...] = jnp.zeros(acc.shape, F32)

    h = h_ref[0]
    a = jnp.dot(h, wg_ref[0], preferred_element_type=F32)
    u = jnp.dot(h, wu_ref[0], preferred_element_type=F32)
    hid = (a * jax.nn.sigmoid(a)) * u
    out = jnp.dot(hid.astype(BF16), wd_ref[0], preferred_element_type=F32)
    pick = jnp.where(lax.broadcasted_iota(jnp.int32, (LANES, LANES), 0) == e, 1.0, 0.0)
    gcol = jnp.dot(gates_ref[0], pick, precision=HIGHEST, preferred_element_type=F32)
    acc[...] += jnp.concatenate([gcol] * (D_MODEL // LANES), axis=1) * out

    @pl.when(e == N_EXPERTS - 1)
    def _():
        o_ref[0] = x1_ref[0] + g2_ref[0] * (_rms(acc[...]) * pg_ref[...])


def _moe(h2, gates, x1, g2, post_ffn_gain, w_gate, w_up, w_down):
    b, s, d = x1.shape
    t = MOE_TILE
    tok = lambda width: pl.BlockSpec((1, t, width), lambda bi, i, e: (bi, i, 0))
    return pl.pallas_call(
        _moe_kernel,
        grid=(b, s // t, N_EXPERTS),
        in_specs=[tok(d), tok(LANES), tok(d),
                  pl.BlockSpec((1, 1, d), lambda bi, i, e: (bi, 0, 0)),
                  pl.BlockSpec((1, d), lambda bi, i, e: (0, 0)),
                  pl.BlockSpec((1, d, EXPERT_FF), lambda bi, i, e: (e, 0, 0)),
                  pl.BlockSpec((1, d, EXPERT_FF), lambda bi, i, e: (e, 0, 0)),
                  pl.BlockSpec((1, EXPERT_FF, d), lambda bi, i, e: (e, 0, 0))],
        out_specs=tok(d),
        out_shape=jax.ShapeDtypeStruct((b, s, d), F32),
        scratch_shapes=[pltpu.VMEM((t, d), F32)],
        compiler_params=_params(3),
        name="moe_experts",
    )(h2, gates, x1, g2, post_ffn_gain, w_gate, w_up, w_down)


def _augment_w_in(w_in):
    d = w_in.shape[0]
    scale = HEAD_DIM ** -0.5 * LOG2E
    dq, dk, dv, mq, mk, mv = jnp.split(w_in, [512, 1024, 1536, 2048, 2560], axis=1)

    def pad_heads(w):
        w = w.reshape(d, N_MOBA_HEADS, HEAD_DIM)
        return jnp.pad(w, ((0, 0), (0, 0), (0, LANES - HEAD_DIM))).reshape(d, N_MOBA_HEADS * LANES)

    return jnp.concatenate([dq * scale, dk, dv, pad_heads(mq * scale), pad_heads(mk), mv],
                           axis=1).astype(BF16)


def kernel(x, c, w_ada, b_ada, pre_mix_gain, post_mix_gain, pre_ffn_gain, post_ffn_gain, w_in, lambda_q1, lambda_k1, lambda_q2, lambda_k2, diff_head_gain, w_out, rel_bias, w_group, b_group, w_expert, b_expert, w_gate, w_up, w_down):
    b, s, d = x.shape
    depth = w_in.shape[0]
    tab_t = rel_bias.T
    bias_diff = _bias_tiles(tab_t, ATTN_TILE, 0, N_DIFF_HEADS)
    bias_moba = _bias_tiles(tab_t, ATTN_TILE, N_DIFF_HEADS, N_MOBA_HEADS)
    c_pad = jnp.pad(c, ((0, 8 - b), (0, 0)))
    for l in range(depth):
        lambda_init = 0.8 - 0.6 * math.exp(-0.3 * l)
        mod = _ada(c_pad, w_ada[l], b_ada[l][None])[:b]
        sh1, sc1, g1, sh2, sc2, g2 = [m[:, None, :] for m in jnp.split(mod, 6, axis=-1)]

        dq, dk, dv, mq, mk, mv, kmean = _inproj(x, pre_mix_gain[l][None], sc1, sh1,
                                                _augment_w_in(w_in[l]))
        y_diff = _diff_attention(dq, dk, dv, bias_diff, lambda_q1[l][None], lambda_k1[l][None],
                                 lambda_q2[l][None], lambda_k2[l][None],
                                 diff_head_gain[l][None], lambda_init)
        nkb = s // MOBA_BLOCK
        km = kmean.reshape(b, nkb, N_MOBA_HEADS, LANES)[..., :HEAD_DIM]
        km = jnp.pad(km.transpose(0, 2, 3, 1),
                     ((0, 0), (0, 0), (0, LANES - HEAD_DIM), (HEAD_DIM, LANES - HEAD_DIM - nkb)))
        y_moba = _moba_attention(mq, mk, mv, km, bias_moba)

        w_router = jnp.pad(jnp.concatenate([w_expert[l], w_group[l]], axis=1),
                           ((0, 0), (0, LANES - N_EXPERTS - N_GROUPS)))
        b_router = jnp.pad(jnp.concatenate([b_expert[l], b_group[l]]),
                           (0, LANES - N_EXPERTS - N_GROUPS))[None]
        x1, h2, gates, cnt, pick, gw, xs = _mid(
            y_diff, y_moba, x, w_out[l].astype(BF16), post_mix_gain[l][None], g1,
            pre_ffn_gain[l][None], sc2, sh2, w_router, b_router)
        wg, wu, wd = w_gate[l].astype(BF16), w_up[l].astype(BF16), w_down[l].astype(BF16)
        gain = post_ffn_gain[l][None]

        def routed(_):
            return _combine(pick, _experts(xs, gw, wg, wu, wd), x1, g2, gain)

        def dense(_):
            return _moe(h2, gates, x1, g2, gain, wg, wu, wd)

        x = lax.cond(jnp.max(cnt) > EXPERT_CAP, dense, routed, None)
    return x
```

```python
import functools
import math

import jax
import jax.numpy as jnp
from jax import lax
from jax.experimental import pallas as pl
from jax.experimental.pallas import tpu as pltpu

F32 = jnp.float32
BF16 = jnp.bfloat16
HIGHEST = lax.Precision.HIGHEST

D_MODEL = 1024
HEAD_DIM = 64
N_DIFF_HEADS = 4
N_MOBA_HEADS = 8
DIFF_COLS = 512
MOBA_COLS = 512
MOBA_BLOCK = 256
MOBA_TOPK = 3
N_BUCKETS = 32
MAX_EXACT = N_BUCKETS // 2
MAX_DISTANCE = 128
N_GROUPS = 4
EXPERTS_PER_GROUP = 8
N_EXPERTS = 32
EXPERT_FF = 512
EPS = 1e-6
NEG = -1e30
LANES = 128
LOG2E = math.log2(math.e)

IN_TILE = 512
ATTN_TILE = 512
MID_TILE = 512
DISPATCH_TILE = 512
EXPERT_TILE = 256
COMBINE_TILE = 256
VMEM_LIMIT = 56 * 1024 * 1024


def _params(n_axes, vmem=VMEM_LIMIT):
    return pltpu.CompilerParams(dimension_semantics=("arbitrary",) * n_axes,
                                vmem_limit_bytes=vmem)


def _rms(v):
    return v * lax.rsqrt(jnp.mean(v * v, axis=-1, keepdims=True) + EPS)


def _ada_kernel(c_ref, w_ref, b_ref, o_ref):
    c = c_ref[...]
    s = c * jax.nn.sigmoid(c)
    o_ref[...] = jnp.dot(s, w_ref[...], precision=HIGHEST,
                         preferred_element_type=F32) + b_ref[...]


def _ada(c_pad, w_ada, b_ada):
    n = w_ada.shape[1]
    return pl.pallas_call(
        _ada_kernel,
        grid=(n // D_MODEL,),
        in_specs=[pl.BlockSpec((8, D_MODEL), lambda j: (0, 0)),
                  pl.BlockSpec((D_MODEL, D_MODEL), lambda j: (0, j)),
                  pl.BlockSpec((1, D_MODEL), lambda j: (0, j))],
        out_specs=pl.BlockSpec((8, D_MODEL), lambda j: (0, j)),
        out_shape=jax.ShapeDtypeStruct((8, n), F32),
        compiler_params=_params(1),
        name="ada_mod",
    )(c_pad, w_ada, b_ada)


def _bias_kernel(tab_ref, o_ref, *, tile, head0):
    w = pl.program_id(0)
    h = pl.program_id(1) + head0
    r = lax.broadcasted_iota(jnp.int32, (tile, tile), 0)
    c = lax.broadcasted_iota(jnp.int32, (tile, tile), 1)
    d = r - c + w * tile
    n = jnp.maximum(d, 0)
    nf = jnp.maximum(n, 1).astype(F32)
    large = MAX_EXACT + (jnp.log(nf / MAX_EXACT) / math.log(MAX_DISTANCE / MAX_EXACT)
                         * (N_BUCKETS - MAX_EXACT)).astype(jnp.int32)
    large = jnp.minimum(large, N_BUCKETS - 1)
    bucket = jnp.where(n < MAX_EXACT, n, large)
    far = tab_ref[h, N_BUCKETS - 1]
    val = jnp.zeros((tile, tile), F32)
    for b in range(N_BUCKETS - 1):
        val = jnp.where(bucket == b, (tab_ref[h, b] - far) * LOG2E, val)
    o_ref[0, 0] = jnp.where(d >= 0, val, NEG)


def _bias_tiles(tab_t, tile, head0, n_heads):
    return pl.pallas_call(
        functools.partial(_bias_kernel, tile=tile, head0=head0),
        grid=(2, n_heads),
        in_specs=[pl.BlockSpec(memory_space=pltpu.SMEM)],
        out_specs=pl.BlockSpec((1, 1, tile, tile), lambda w, h: (w, h, 0, 0)),
        out_shape=jax.ShapeDtypeStruct((2, n_heads, tile, tile), F32),
        compiler_params=_params(2),
        name="bias_tiles",
    )(tab_t)


W_DQ, W_DK, W_DV = 0, 512, 1024
W_MQ, W_MK, W_MV = 1536, 2560, 3584
W_END = 4096


def _inproj_kernel(x_ref, gain_ref, sc_ref, sh_ref, w_ref,
                   dq_ref, dk_ref, dv_ref, mq_ref, mk_ref, mv_ref, km_ref):
    i = pl.program_id(1)
    h = _rms(x_ref[0]) * gain_ref[...]
    h = h * (1.0 + sc_ref[0]) + sh_ref[0]
    hb = h.astype(BF16)

    def proj(lo, hi):
        return jnp.dot(hb, w_ref[:, lo:hi], preferred_element_type=F32)

    dq_ref[0] = proj(W_DQ, W_DK).astype(BF16)
    dk_ref[0] = proj(W_DK, W_DV).astype(BF16)
    dv_ref[0] = proj(W_DV, W_MQ).astype(BF16)
    mq_ref[0] = proj(W_MQ, W_MK).astype(BF16)
    mv_ref[0] = proj(W_MV, W_END).astype(BF16)
    mk = proj(W_MK, W_MV)
    for r in range(IN_TILE // MOBA_BLOCK):
        km_ref[0, r] = jnp.mean(mk[r * MOBA_BLOCK:(r + 1) * MOBA_BLOCK], axis=0, keepdims=True)
    row = lax.broadcasted_iota(jnp.int32, mk.shape, 0)
    lane = lax.broadcasted_iota(jnp.int32, mk.shape, 1)
    blk = (i * IN_TILE + row) // MOBA_BLOCK
    onehot = (lane % LANES) - HEAD_DIM == blk
    mk_ref[0] = jnp.where(onehot, 1.0, mk).astype(BF16)


def _inproj(x, gain, sc1, sh1, w_aug):
    b, s, d = x.shape
    nkb = s // MOBA_BLOCK
    tok = lambda width: pl.BlockSpec((1, IN_TILE, width), lambda bi, i: (bi, i, 0))
    vec = pl.BlockSpec((1, 1, d), lambda bi, i: (bi, 0, 0))
    shp = lambda width: jax.ShapeDtypeStruct((b, s, width), BF16)
    return pl.pallas_call(
        _inproj_kernel,
        grid=(b, s // IN_TILE),
        in_specs=[tok(d), pl.BlockSpec((1, d), lambda bi, i: (0, 0)), vec, vec,
                  pl.BlockSpec((d, W_END), lambda bi, i: (0, 0))],
        out_specs=[tok(512), tok(512), tok(512), tok(1024), tok(1024), tok(512),
                   pl.BlockSpec((1, IN_TILE // MOBA_BLOCK, 1, 1024), lambda bi, i: (bi, i, 0, 0))],
        out_shape=[shp(512), shp(512), shp(512), shp(1024), shp(1024), shp(512),
                   jax.ShapeDtypeStruct((b, nkb, 1, 1024), F32)],
        compiler_params=_params(2),
        name="in_proj",
    )(x, gain, sc1, sh1, w_aug)


def _softmax_step(s, v, m_ref, l_ref, acc_ref, idx):
    groups = [s[:, g * LANES:(g + 1) * LANES] for g in range(s.shape[1] // LANES)]
    m_prev = m_ref[idx]
    m_tile = functools.reduce(jnp.maximum, groups)
    m_next = jnp.maximum(m_prev, jnp.max(m_tile, axis=1, keepdims=True))
    alpha = jnp.exp2(m_prev - m_next)
    ps = [jnp.exp2(g - m_next) for g in groups]
    l_ref[idx] = alpha * l_ref[idx] + functools.reduce(jnp.add, ps)
    p = jnp.concatenate(ps, axis=1).astype(BF16)
    acc_ref[idx] = alpha * acc_ref[idx] + jnp.dot(p, v, preferred_element_type=F32)
    m_ref[idx] = m_next


def _softmax_result(l_ref, acc_ref, idx):
    return acc_ref[idx] / jnp.sum(l_ref[idx], axis=1, keepdims=True)


def _nt_dot(a, b):
    return lax.dot_general(a, b, (((1,), (1,)), ((), ())), preferred_element_type=F32)


def _init_softmax_state(m_s, l_s, acc_s):
    m_s[...] = jnp.full(m_s.shape, -jnp.inf, F32)
    l_s[...] = jnp.zeros(l_s.shape, F32)
    acc_s[...] = jnp.zeros(acc_s.shape, F32)


def _causal_sweep(i, step):
    step(i, 0)

    @pl.when(i > 0)
    def _():
        step(i - 1, 1)

    n_far = jnp.maximum(i - 1, 0)

    def far_pair(jj, carry):
        step(2 * jj, None)
        step(2 * jj + 1, None)
        return carry

    lax.fori_loop(0, n_far // 2, far_pair, 0)

    @pl.when(n_far % 2 == 1)
    def _():
        step(n_far - 1, None)


def _attn_scratch():
    t = ATTN_TILE
    return [pltpu.VMEM((2, t, LANES), BF16), pltpu.VMEM((2, t, LANES), F32),
            pltpu.VMEM((2, t, LANES), F32), pltpu.VMEM((2, t, LANES), F32)]


def _diff_kernel(lq1_ref, lk1_ref, lq2_ref, lk2_ref, q_ref, k_ref, v_ref, bias_ref, gain_ref,
                 o_ref, qs, m_s, l_s, acc_s, *, lambda_init):
    i = pl.program_id(2)
    t = ATTN_TILE
    q = q_ref[0]
    lane = lax.broadcasted_iota(jnp.int32, q.shape, 1)
    zero = jnp.zeros_like(q)
    qs[0] = jnp.where(lane < HEAD_DIM, q, zero)
    qs[1] = jnp.where(lane >= HEAD_DIM, q, zero)
    _init_softmax_state(m_s, l_s, acc_s)

    def step(j, which):
        off = pl.multiple_of(j * t, t)
        k = k_ref[0, pl.ds(off, t), :]
        v = v_ref[0, pl.ds(off, t), :]
        for m in range(2):
            s = _nt_dot(qs[m], k)
            if which is not None:
                s = s + bias_ref[which, 0]
            _softmax_step(s, v, m_s, l_s, acc_s, m)

    _causal_sweep(i, step)

    lam = (jnp.exp(jnp.sum(lq1_ref[...] * lk1_ref[...], axis=-1, keepdims=True))
           - jnp.exp(jnp.sum(lq2_ref[...] * lk2_ref[...], axis=-1, keepdims=True)) + lambda_init)
    a = _softmax_result(l_s, acc_s, 0) - lam * _softmax_result(l_s, acc_s, 1)
    y = _rms(a) * gain_ref[...] * (1.0 - lambda_init)
    o_ref[0] = y.astype(BF16)


def _diff_attention(dq, dk, dv, bias, lq1, lk1, lq2, lk2, head_gain, lambda_init):
    b, s, _ = dq.shape
    t = ATTN_TILE
    lam_spec = pl.BlockSpec((1, HEAD_DIM), lambda bi, h, i: (0, 0))
    full = pl.BlockSpec((1, s, LANES), lambda bi, h, i: (bi, 0, h))
    return pl.pallas_call(
        functools.partial(_diff_kernel, lambda_init=lambda_init),
        grid=(b, N_DIFF_HEADS, s // t),
        in_specs=[lam_spec, lam_spec, lam_spec, lam_spec,
                  pl.BlockSpec((1, t, LANES), lambda bi, h, i: (bi, i, h)),
                  full, full,
                  pl.BlockSpec((2, 1, t, t), lambda bi, h, i: (0, h, 0, 0)),
                  pl.BlockSpec((1, LANES), lambda bi, h, i: (0, 0))],
        out_specs=pl.BlockSpec((1, t, LANES), lambda bi, h, i: (bi, i, h)),
        out_shape=jax.ShapeDtypeStruct((b, s, DIFF_COLS), BF16),
        scratch_shapes=_attn_scratch(),
        compiler_params=_params(3),
        name="diff_attn",
    )(lq1, lk1, lq2, lk2, dq, dk, dv, bias, head_gain)


def _moba_kernel(q_ref, k_ref, v_ref, km_ref, bias_ref, o_ref, qs, m_s, l_s, acc_s):
    i = pl.program_id(2)
    t = ATTN_TILE
    lane = lax.broadcasted_iota(jnp.int32, (t, LANES), 1)
    row = lax.broadcasted_iota(jnp.int32, (t, LANES), 0)
    blk = lane - HEAD_DIM
    own = i * (t // MOBA_BLOCK) + row // MOBA_BLOCK
    for hh in range(2):
        qa = q_ref[0, :, hh * LANES:(hh + 1) * LANES]
        g = jnp.dot(qa.astype(F32), km_ref[0, hh], precision=HIGHEST,
                    preferred_element_type=F32)
        g = jnp.where((blk >= 0) & (blk < own), g, -jnp.inf)
        sel = blk == own
        for _ in range(MOBA_TOPK):
            mx = jnp.max(g, axis=1, keepdims=True)
            first = jnp.min(jnp.where(g == mx, lane, 2 * LANES), axis=1, keepdims=True)
            pick = (lane == first) & (mx > -jnp.inf)
            sel = sel | pick
            g = jnp.where(pick, -jnp.inf, g)
        gate = jnp.where(sel, 0.0, NEG).astype(BF16)
        qs[hh] = jnp.where(lane < HEAD_DIM, qa, gate)
    _init_softmax_state(m_s, l_s, acc_s)

    def step(j, which):
        off = pl.multiple_of(j * t, t)
        k = k_ref[0, pl.ds(off, t), :]
        v = v_ref[0, pl.ds(off, t), :]
        for hh in range(2):
            s = _nt_dot(qs[hh], k[:, hh * LANES:(hh + 1) * LANES])
            if which is not None:
                s = s + bias_ref[which, hh]
            _softmax_step(s, v, m_s, l_s, acc_s, hh)

    _causal_sweep(i, step)

    o = jnp.where(lane < HEAD_DIM, _softmax_result(l_s, acc_s, 0), _softmax_result(l_s, acc_s, 1))
    o_ref[0] = o.astype(BF16)


def _moba_attention(mq, mk, mv, km, bias):
    b, s, _ = mq.shape
    t = ATTN_TILE
    return pl.pallas_call(
        _moba_kernel,
        grid=(b, N_MOBA_HEADS // 2, s // t),
        in_specs=[pl.BlockSpec((1, t, 2 * LANES), lambda bi, p, i: (bi, i, p)),
                  pl.BlockSpec((1, s, 2 * LANES), lambda bi, p, i: (bi, 0, p)),
                  pl.BlockSpec((1, s, LANES), lambda bi, p, i: (bi, 0, p)),
                  pl.BlockSpec((1, 2, LANES, LANES), lambda bi, p, i: (bi, p, 0, 0)),
                  pl.BlockSpec((2, 2, t, t), lambda bi, p, i: (0, p, 0, 0))],
        out_specs=pl.BlockSpec((1, t, LANES), lambda bi, p, i: (bi, i, p)),
        out_shape=jax.ShapeDtypeStruct((b, s, MOBA_COLS), BF16),
        scratch_shapes=_attn_scratch(),
        compiler_params=_params(3),
        name="moba_attn",
    )(mq, mk, mv, km, bias)


ROUTER_GROUP_LANE = N_EXPERTS
ROUTE_E1, ROUTE_E2, ROUTE_R1, ROUTE_R2, ROUTE_W1, ROUTE_W2 = range(6)


def _mid_kernel(yd_ref, ym_ref, x_ref, wo_ref, pg_ref, g1_ref, fg_ref, sc2_ref, sh2_ref,
                wr_ref, br_ref, x1_ref, h2_ref, route_ref, tot_ref, run):
    first = (pl.program_id(0) == 0) & (pl.program_id(1) == 0)

    @pl.when(first)
    def _():
        run[...] = jnp.zeros(run.shape, F32)

    y = (jnp.dot(yd_ref[0], wo_ref[0:DIFF_COLS], preferred_element_type=F32)
         + jnp.dot(ym_ref[0], wo_ref[DIFF_COLS:D_MODEL], preferred_element_type=F32))
    x1 = x_ref[0] + g1_ref[0] * (_rms(y) * pg_ref[...])
    x1_ref[0] = x1
    h2 = _rms(x1) * fg_ref[...] * (1.0 + sc2_ref[0]) + sh2_ref[0]
    h2_ref[0] = h2

    logits = jnp.dot(h2, wr_ref[...], precision=HIGHEST, preferred_element_type=F32) + br_ref[...]
    lane = lax.broadcasted_iota(jnp.int32, logits.shape, 1)
    is_group = (lane >= ROUTER_GROUP_LANE) & (lane < ROUTER_GROUP_LANE + N_GROUPS)
    gl = jnp.where(is_group, logits, -jnp.inf)
    gmax = jnp.max(gl, axis=1, keepdims=True)
    g_idx = jnp.min(jnp.where(gl == gmax, lane - ROUTER_GROUP_LANE, N_GROUPS), axis=1, keepdims=True)
    g_w = 1.0 / jnp.sum(jnp.exp(gl - gmax), axis=1, keepdims=True)
    in_group = (lane < N_EXPERTS) & (lane // EXPERTS_PER_GROUP == g_idx)
    el = jnp.where(in_group, logits, -jnp.inf)
    m1 = jnp.max(el, axis=1, keepdims=True)
    i1 = jnp.min(jnp.where(el == m1, lane, LANES), axis=1, keepdims=True)
    el2 = jnp.where(lane == i1, -jnp.inf, el)
    m2 = jnp.max(el2, axis=1, keepdims=True)
    i2 = jnp.min(jnp.where(el2 == m2, lane, LANES), axis=1, keepdims=True)
    e2 = jnp.exp(m2 - m1)
    w1 = g_w / (1.0 + e2)
    w2 = g_w * e2 / (1.0 + e2)

    t = MID_TILE
    used = jnp.where((lane == i1) | (lane == i2), 1.0, 0.0)
    earlier = jnp.where(lax.broadcasted_iota(jnp.int32, (t, t), 1)
                        < lax.broadcasted_iota(jnp.int32, (t, t), 0), 1.0, 0.0).astype(BF16)
    rank = jnp.dot(earlier, used.astype(BF16), preferred_element_type=F32) + run[...]
    r1 = jnp.sum(jnp.where(lane == i1, rank, 0.0), axis=1, keepdims=True)
    r2 = jnp.sum(jnp.where(lane == i2, rank, 0.0), axis=1, keepdims=True)
    record = jnp.zeros(logits.shape, F32)
    for field, val in ((ROUTE_E1, i1.astype(F32)), (ROUTE_E2, i2.astype(F32)), (ROUTE_R1, r1),
                       (ROUTE_R2, r2), (ROUTE_W1, w1), (ROUTE_W2, w2)):
        record = jnp.where(lane == field, val, record)
    route_ref[0] = record
    run[...] += jnp.sum(used, axis=0, keepdims=True)
    tot_ref[...] = run[...]


def _mid(yd, ym, x, w_out, post_mix_gain, g1, pre_ffn_gain, sc2, sh2, w_router, b_router):
    b, s, d = x.shape
    t = MID_TILE
    tok = lambda width: pl.BlockSpec((1, t, width), lambda bi, i: (bi, i, 0))
    vec = pl.BlockSpec((1, 1, d), lambda bi, i: (bi, 0, 0))
    row = lambda width: pl.BlockSpec((1, width), lambda bi, i: (0, 0))
    return pl.pallas_call(
        _mid_kernel,
        grid=(b, s // t),
        in_specs=[tok(DIFF_COLS), tok(MOBA_COLS), tok(d),
                  pl.BlockSpec((d, d), lambda bi, i: (0, 0)),
                  row(d), vec, row(d), vec, vec,
                  pl.BlockSpec((d, LANES), lambda bi, i: (0, 0)), row(LANES)],
        out_specs=[tok(d), tok(d), tok(LANES), row(LANES)],
        out_shape=[jax.ShapeDtypeStruct((b, s, d), F32),
                   jax.ShapeDtypeStruct((b, s, d), F32),
                   jax.ShapeDtypeStruct((b, s, LANES), F32),
                   jax.ShapeDtypeStruct((1, LANES), F32)],
        scratch_shapes=[pltpu.VMEM((1, LANES), F32)],
        compiler_params=_params(2),
        name="out_proj_router",
    )(yd, ym, x, w_out, post_mix_gain, g1, pre_ffn_gain, sc2, sh2, w_router, b_router)


def _row_copy(src, src_row, dst, dst_row, sem):
    return pltpu.make_async_copy(src.at[pl.ds(src_row, 1), :], dst.at[pl.ds(dst_row, 1), :], sem)


def _dispatch_kernel(pos_ref, h_ref, xs_in_ref, xs_ref, sem):
    del xs_in_ref
    base = pl.program_id(0) * DISPATCH_TILE

    def issue(t, carry):
        for k in range(2):
            _row_copy(h_ref, base + t, xs_ref, pos_ref[0, 0, 2 * t + k], sem).start()
        return carry

    lax.fori_loop(0, DISPATCH_TILE, issue, 0)

    def drain(t, carry):
        for k in range(2):
            _row_copy(h_ref, base + t, xs_ref, pos_ref[0, 0, 2 * t + k], sem).wait()
        return carry

    lax.fori_loop(0, DISPATCH_TILE, drain, 0)


def _dispatch(pos, h2, xs_zero):
    n, d = h2.shape
    t = DISPATCH_TILE
    return pl.pallas_call(
        _dispatch_kernel,
        grid=(n // t,),
        in_specs=[pl.BlockSpec((1, 1, 2 * t), lambda i: (i, 0, 0), memory_space=pltpu.SMEM),
                  pl.BlockSpec(memory_space=pl.ANY),
                  pl.BlockSpec(memory_space=pl.ANY)],
        out_specs=pl.BlockSpec(memory_space=pl.ANY),
        out_shape=jax.ShapeDtypeStruct(xs_zero.shape, F32),
        scratch_shapes=[pltpu.SemaphoreType.DMA(())],
        input_output_aliases={2: 0},
        compiler_params=_params(1),
        name="moe_dispatch",
    )(pos.reshape(n // t, 1, 2 * t), h2, xs_zero)


def _expert_kernel(te_ref, x_ref, wg_ref, wu_ref, wd_ref, y_ref):
    del te_ref
    x = x_ref[...].astype(BF16)
    a = jnp.dot(x, wg_ref[0], preferred_element_type=F32)
    u = jnp.dot(x, wu_ref[0], preferred_element_type=F32)
    hid = (a * jax.nn.sigmoid(a)) * u
    y_ref[...] = jnp.dot(hid.astype(BF16), wd_ref[0], preferred_element_type=F32)


def _experts(tile_expert, xs, w_gate, w_up, w_down):
    rows, d = xs.shape
    t = EXPERT_TILE
    return pl.pallas_call(
        _expert_kernel,
        grid_spec=pltpu.PrefetchScalarGridSpec(
            num_scalar_prefetch=1,
            grid=(rows // t,),
            in_specs=[pl.BlockSpec((t, d), lambda j, te: (j, 0)),
                      pl.BlockSpec((1, d, EXPERT_FF), lambda j, te: (te[j], 0, 0)),
                      pl.BlockSpec((1, d, EXPERT_FF), lambda j, te: (te[j], 0, 0)),
                      pl.BlockSpec((1, EXPERT_FF, d), lambda j, te: (te[j], 0, 0))],
            out_specs=pl.BlockSpec((t, d), lambda j, te: (j, 0))),
        out_shape=jax.ShapeDtypeStruct((rows, d), F32),
        compiler_params=_params(1),
        name="routed_experts",
    )(tile_expert, xs, w_gate, w_up, w_down)


def _combine_kernel(pos_ref, y_ref, route_ref, x1_ref, g2_ref, pg_ref, o_ref, buf, sem):
    t = COMBINE_TILE

    def issue(r, carry):
        for k in range(2):
            _row_copy(y_ref, pos_ref[0, 0, 2 * r + k], buf.at[k], r, sem).start()
        return carry

    lax.fori_loop(0, t, issue, 0)

    def drain(r, carry):
        for k in range(2):
            _row_copy(y_ref, pos_ref[0, 0, 2 * r + k], buf.at[k], r, sem).wait()
        return carry

    lax.fori_loop(0, t, drain, 0)

    route = route_ref[0]
    w1 = route[:, ROUTE_W1:ROUTE_W1 + 1]
    w2 = route[:, ROUTE_W2:ROUTE_W2 + 1]
    y = w1 * buf[0] + w2 * buf[1]
    o_ref[0] = x1_ref[0] + g2_ref[0] * (_rms(y) * pg_ref[...])


def _combine(pos, ys, route, x1, g2, post_ffn_gain):
    b, s, d = x1.shape
    t = COMBINE_TILE
    nt = s // t
    tok = lambda width: pl.BlockSpec((1, t, width), lambda bi, i: (bi, i, 0))
    return pl.pallas_call(
        _combine_kernel,
        grid=(b, nt),
        in_specs=[pl.BlockSpec((1, 1, 2 * t), lambda bi, i: (bi * nt + i, 0, 0),
                               memory_space=pltpu.SMEM),
                  pl.BlockSpec(memory_space=pl.ANY),
                  tok(LANES), tok(d),
                  pl.BlockSpec((1, 1, d), lambda bi, i: (bi, 0, 0)),
                  pl.BlockSpec((1, d), lambda bi, i: (0, 0))],
        out_specs=tok(d),
        out_shape=jax.ShapeDtypeStruct((b, s, d), F32),
        scratch_shapes=[pltpu.VMEM((2, t, d), F32), pltpu.SemaphoreType.DMA(())],
        compiler_params=_params(2),
        name="moe_combine",
    )(pos.reshape(b * nt, 1, 2 * t), ys, route, x1, g2, post_ffn_gain)


def _routing_tables(route, totals, n_rows):
    t = EXPERT_TILE
    tot = totals[0, :N_EXPERTS].astype(jnp.int32)
    padded = (tot + t - 1) // t * t
    ends = jnp.cumsum(padded)
    base = ends - padded
    rec = route.reshape(-1, LANES)
    e = rec[:, ROUTE_E1:ROUTE_E2 + 1].astype(jnp.int32)
    r = rec[:, ROUTE_R1:ROUTE_R2 + 1].astype(jnp.int32)
    pos = base[e] + r
    tile_start = jnp.arange(n_rows // t, dtype=jnp.int32) * t
    tile_expert = jnp.minimum(jnp.searchsorted(ends, tile_start, side="right"),
                              N_EXPERTS - 1).astype(jnp.int32)
    return pos.reshape(-1), tile_expert


def _augment_w_in(w_in):
    d = w_in.shape[0]
    scale = HEAD_DIM ** -0.5 * LOG2E
    dq, dk, dv, mq, mk, mv = jnp.split(w_in, [512, 1024, 1536, 2048, 2560], axis=1)

    def pad_heads(w):
        w = w.reshape(d, N_MOBA_HEADS, HEAD_DIM)
        return jnp.pad(w, ((0, 0), (0, 0), (0, LANES - HEAD_DIM))).reshape(d, N_MOBA_HEADS * LANES)

    return jnp.concatenate([dq * scale, dk, dv, pad_heads(mq * scale), pad_heads(mk), mv],
                           axis=1).astype(BF16)


def kernel(x, c, w_ada, b_ada, pre_mix_gain, post_mix_gain, pre_ffn_gain, post_ffn_gain, w_in, lambda_q1, lambda_k1, lambda_q2, lambda_k2, diff_head_gain, w_out, rel_bias, w_group, b_group, w_expert, b_expert, w_gate, w_up, w_down):
    b, s, d = x.shape
    depth = w_in.shape[0]
    tab_t = rel_bias.T
    bias_diff = _bias_tiles(tab_t, ATTN_TILE, 0, N_DIFF_HEADS)
    bias_moba = _bias_tiles(tab_t, ATTN_TILE, N_DIFF_HEADS, N_MOBA_HEADS)
    c_pad = jnp.pad(c, ((0, 8 - b), (0, 0)))
    for l in range(depth):
        lambda_init = 0.8 - 0.6 * math.exp(-0.3 * l)
        mod = _ada(c_pad, w_ada[l], b_ada[l][None])[:b]
        sh1, sc1, g1, sh2, sc2, g2 = [m[:, None, :] for m in jnp.split(mod, 6, axis=-1)]

        dq, dk, dv, mq, mk, mv, kmean = _inproj(x, pre_mix_gain[l][None], sc1, sh1,
                                                _augment_w_in(w_in[l]))
        y_diff = _diff_attention(dq, dk, dv, bias_diff, lambda_q1[l][None], lambda_k1[l][None],
                                 lambda_q2[l][None], lambda_k2[l][None],
                                 diff_head_gain[l][None], lambda_init)
        nkb = s // MOBA_BLOCK
        km = kmean.reshape(b, nkb, N_MOBA_HEADS, LANES)[..., :HEAD_DIM]
        km = jnp.pad(km.transpose(0, 2, 3, 1),
                     ((0, 0), (0, 0), (0, LANES - HEAD_DIM), (HEAD_DIM, LANES - HEAD_DIM - nkb)))
        y_moba = _moba_attention(mq, mk, mv, km, bias_moba)

        w_router = jnp.pad(jnp.concatenate([w_expert[l], w_group[l]], axis=1),
                           ((0, 0), (0, LANES - N_EXPERTS - N_GROUPS)))
        b_router = jnp.pad(jnp.concatenate([b_expert[l], b_group[l]]),
                           (0, LANES - N_EXPERTS - N_GROUPS))[None]
        x1, h2, route, totals = _mid(
            y_diff, y_moba, x, w_out[l].astype(BF16), post_mix_gain[l][None], g1,
            pre_ffn_gain[l][None], sc2, sh2, w_router, b_router)
        n_rows = 2 * b * s + N_EXPERTS * EXPERT_TILE
        pos, tile_expert = _routing_tables(route, totals, n_rows)
        xs = _dispatch(pos, h2.reshape(b * s, d), jnp.zeros((n_rows, d), F32))
        ys = _experts(tile_expert, xs, w_gate[l].astype(BF16), w_up[l].astype(BF16),
                      w_down[l].astype(BF16))
        x = _combine(pos, ys, route, x1, g2, post_ffn_gain[l][None])
    return x
```

```python
import functools
import math

import jax
import jax.numpy as jnp
from jax import lax
from jax.experimental import pallas as pl
from jax.experimental.pallas import tpu as pltpu

F32 = jnp.float32
BF16 = jnp.bfloat16
HIGHEST = lax.Precision.HIGHEST

D_MODEL = 1024
HEAD_DIM = 64
N_DIFF_HEADS = 4
N_MOBA_HEADS = 8
DIFF_COLS = 512
MOBA_COLS = 512
MOBA_BLOCK = 256
MOBA_TOPK = 3
N_BUCKETS = 32
MAX_EXACT = N_BUCKETS // 2
MAX_DISTANCE = 128
N_GROUPS = 4
EXPERTS_PER_GROUP = 8
N_EXPERTS = 32
EXPERT_FF = 512
EPS = 1e-6
NEG = -1e30
LANES = 128
LOG2E = math.log2(math.e)

IN_TILE = 512
ATTN_TILE = 512
MID_TILE = 512
DISPATCH_TILE = 512
EXPERT_TILE = 256
COMBINE_TILE = 256
VMEM_LIMIT = 56 * 1024 * 1024


def _params(n_axes, vmem=VMEM_LIMIT):
    return pltpu.CompilerParams(dimension_semantics=("arbitrary",) * n_axes,
                                vmem_limit_bytes=vmem)


def _rms(v):
    return v * lax.rsqrt(jnp.mean(v * v, axis=-1, keepdims=True) + EPS)


def _ada_kernel(c_ref, w_ref, b_ref, o_ref):
    c = c_ref[...]
    s = c * jax.nn.sigmoid(c)
    o_ref[...] = jnp.dot(s, w_ref[...], precision=HIGHEST,
                         preferred_element_type=F32) + b_ref[...]


def _ada(c_pad, w_ada, b_ada):
    n = w_ada.shape[1]
    return pl.pallas_call(
        _ada_kernel,
        grid=(n // D_MODEL,),
        in_specs=[pl.BlockSpec((8, D_MODEL), lambda j: (0, 0)),
                  pl.BlockSpec((D_MODEL, D_MODEL), lambda j: (0, j)),
                  pl.BlockSpec((1, D_MODEL), lambda j: (0, j))],
        out_specs=pl.BlockSpec((8, D_MODEL), lambda j: (0, j)),
        out_shape=jax.ShapeDtypeStruct((8, n), F32),
        compiler_params=_params(1),
        name="ada_mod",
    )(c_pad, w_ada, b_ada)


def _bias_kernel(tab_ref, o_ref, *, tile, head0):
    w = pl.program_id(0)
    h = pl.program_id(1) + head0
    r = lax.broadcasted_iota(jnp.int32, (tile, tile), 0)
    c = lax.broadcasted_iota(jnp.int32, (tile, tile), 1)
    d = r - c + w * tile
    n = jnp.maximum(d, 0)
    nf = jnp.maximum(n, 1).astype(F32)
    large = MAX_EXACT + (jnp.log(nf / MAX_EXACT) / math.log(MAX_DISTANCE / MAX_EXACT)
                         * (N_BUCKETS - MAX_EXACT)).astype(jnp.int32)
    large = jnp.minimum(large, N_BUCKETS - 1)
    bucket = jnp.where(n < MAX_EXACT, n, large)
    far = tab_ref[h, N_BUCKETS - 1]
    val = jnp.zeros((tile, tile), F32)
    for b in range(N_BUCKETS - 1):
        val = jnp.where(bucket == b, (tab_ref[h, b] - far) * LOG2E, val)
    o_ref[0, 0] = jnp.where(d >= 0, val, NEG)


def _bias_tiles(tab_t, tile, head0, n_heads):
    return pl.pallas_call(
        functools.partial(_bias_kernel, tile=tile, head0=head0),
        grid=(2, n_heads),
        in_specs=[pl.BlockSpec(memory_space=pltpu.SMEM)],
        out_specs=pl.BlockSpec((1, 1, tile, tile), lambda w, h: (w, h, 0, 0)),
        out_shape=jax.ShapeDtypeStruct((2, n_heads, tile, tile), F32),
        compiler_params=_params(2),
        name="bias_tiles",
    )(tab_t)


W_DQ, W_DK, W_DV = 0, 512, 1024
W_MQ, W_MK, W_MV = 1536, 2560, 3584
W_END = 4096


def _inproj_kernel(x_ref, gain_ref, sc_ref, sh_ref, w_ref,
                   dq_ref, dk_ref, dv_ref, mq_ref, mk_ref, mv_ref, km_ref):
    i = pl.program_id(1)
    h = _rms(x_ref[0]) * gain_ref[...]
    h = h * (1.0 + sc_ref[0]) + sh_ref[0]
    hb = h.astype(BF16)

    def proj(lo, hi):
        return jnp.dot(hb, w_ref[:, lo:hi], preferred_element_type=F32)

    dq_ref[0] = proj(W_DQ, W_DK).astype(BF16)
    dk_ref[0] = proj(W_DK, W_DV).astype(BF16)
    dv_ref[0] = proj(W_DV, W_MQ).astype(BF16)
    mq_ref[0] = proj(W_MQ, W_MK).astype(BF16)
    mv_ref[0] = proj(W_MV, W_END).astype(BF16)
    mk = proj(W_MK, W_MV)
    for r in range(IN_TILE // MOBA_BLOCK):
        km_ref[0, r] = jnp.mean(mk[r * MOBA_BLOCK:(r + 1) * MOBA_BLOCK], axis=0, keepdims=True)
    row = lax.broadcasted_iota(jnp.int32, mk.shape, 0)
    lane = lax.broadcasted_iota(jnp.int32, mk.shape, 1)
    blk = (i * IN_TILE + row) // MOBA_BLOCK
    onehot = (lane % LANES) - HEAD_DIM == blk
    mk_ref[0] = jnp.where(onehot, 1.0, mk).astype(BF16)


def _inproj(x, gain, sc1, sh1, w_aug):
    b, s, d = x.shape
    nkb = s // MOBA_BLOCK
    tok = lambda width: pl.BlockSpec((1, IN_TILE, width), lambda bi, i: (bi, i, 0))
    vec = pl.BlockSpec((1, 1, d), lambda bi, i: (bi, 0, 0))
    shp = lambda width: jax.ShapeDtypeStruct((b, s, width), BF16)
    return pl.pallas_call(
        _inproj_kernel,
        grid=(b, s // IN_TILE),
        in_specs=[tok(d), pl.BlockSpec((1, d), lambda bi, i: (0, 0)), vec, vec,
                  pl.BlockSpec((d, W_END), lambda bi, i: (0, 0))],
        out_specs=[tok(512), tok(512), tok(512), tok(1024), tok(1024), tok(512),
                   pl.BlockSpec((1, IN_TILE // MOBA_BLOCK, 1, 1024), lambda bi, i: (bi, i, 0, 0))],
        out_shape=[shp(512), shp(512), shp(512), shp(1024), shp(1024), shp(512),
                   jax.ShapeDtypeStruct((b, nkb, 1, 1024), F32)],
        compiler_params=_params(2),
        name="in_proj",
    )(x, gain, sc1, sh1, w_aug)


def _softmax_step(s, v, m_ref, l_ref, acc_ref, idx):
    groups = [s[:, g * LANES:(g + 1) * LANES] for g in range(s.shape[1] // LANES)]
    m_prev = m_ref[idx]
    m_tile = functools.reduce(jnp.maximum, groups)
    m_next = jnp.maximum(m_prev, jnp.max(m_tile, axis=1, keepdims=True))
    alpha = jnp.exp2(m_prev - m_next)
    ps = [jnp.exp2(g - m_next) for g in groups]
    l_ref[idx] = alpha * l_ref[idx] + functools.reduce(jnp.add, ps)
    p = jnp.concatenate(ps, axis=1).astype(BF16)
    acc_ref[idx] = alpha * acc_ref[idx] + jnp.dot(p, v, preferred_element_type=F32)
    m_ref[idx] = m_next


def _softmax_result(l_ref, acc_ref, idx):
    return acc_ref[idx] / jnp.sum(l_ref[idx], axis=1, keepdims=True)


def _nt_dot(a, b):
    return lax.dot_general(a, b, (((1,), (1,)), ((), ())), preferred_element_type=F32)


def _init_softmax_state(m_s, l_s, acc_s):
    m_s[...] = jnp.full(m_s.shape, -jnp.inf, F32)
    l_s[...] = jnp.zeros(l_s.shape, F32)
    acc_s[...] = jnp.zeros(acc_s.shape, F32)


def _causal_sweep(i, step):
    step(i, 0)

    @pl.when(i > 0)
    def _():
        step(i - 1, 1)

    n_far = jnp.maximum(i - 1, 0)

    def far_pair(jj, carry):
        step(2 * jj, None)
        step(2 * jj + 1, None)
        return carry

    lax.fori_loop(0, n_far // 2, far_pair, 0)

    @pl.when(n_far % 2 == 1)
    def _():
        step(n_far - 1, None)


def _attn_scratch():
    t = ATTN_TILE
    return [pltpu.VMEM((2, t, LANES), BF16), pltpu.VMEM((2, t, LANES), F32),
            pltpu.VMEM((2, t, LANES), F32), pltpu.VMEM((2, t, LANES), F32)]


def _diff_kernel(lq1_ref, lk1_ref, lq2_ref, lk2_ref, q_ref, k_ref, v_ref, bias_ref, gain_ref,
                 o_ref, qs, m_s, l_s, acc_s, *, lambda_init):
    i = pl.program_id(2)
    t = ATTN_TILE
    q = q_ref[0]
    lane = lax.broadcasted_iota(jnp.int32, q.shape, 1)
    zero = jnp.zeros_like(q)
    qs[0] = jnp.where(lane < HEAD_DIM, q, zero)
    qs[1] = jnp.where(lane >= HEAD_DIM, q, zero)
    _init_softmax_state(m_s, l_s, acc_s)

    def step(j, which):
        off = pl.multiple_of(j * t, t)
        k = k_ref[0, pl.ds(off, t), :]
        v = v_ref[0, pl.ds(off, t), :]
        for m in range(2):
            s = _nt_dot(qs[m], k)
            if which is not None:
                s = s + bias_ref[which, 0]
            _softmax_step(s, v, m_s, l_s, acc_s, m)

    _causal_sweep(i, step)

    lam = (jnp.exp(jnp.sum(lq1_ref[...] * lk1_ref[...], axis=-1, keepdims=True))
           - jnp.exp(jnp.sum(lq2_ref[...] * lk2_ref[...], axis=-1, keepdims=True)) + lambda_init)
    a = _softmax_result(l_s, acc_s, 0) - lam * _softmax_result(l_s, acc_s, 1)
    y = _rms(a) * gain_ref[...] * (1.0 - lambda_init)
    o_ref[0] = y.astype(BF16)


def _diff_attention(dq, dk, dv, bias, lq1, lk1, lq2, lk2, head_gain, lambda_init):
    b, s, _ = dq.shape
    t = ATTN_TILE
    lam_spec = pl.BlockSpec((1, HEAD_DIM), lambda bi, h, i: (0, 0))
    full = pl.BlockSpec((1, s, LANES), lambda bi, h, i: (bi, 0, h))
    return pl.pallas_call(
        functools.partial(_diff_kernel, lambda_init=lambda_init),
        grid=(b, N_DIFF_HEADS, s // t),
        in_specs=[lam_spec, lam_spec, lam_spec, lam_spec,
                  pl.BlockSpec((1, t, LANES), lambda bi, h, i: (bi, i, h)),
                  full, full,
                  pl.BlockSpec((2, 1, t, t), lambda bi, h, i: (0, h, 0, 0)),
                  pl.BlockSpec((1, LANES), lambda bi, h, i: (0, 0))],
        out_specs=pl.BlockSpec((1, t, LANES), lambda bi, h, i: (bi, i, h)),
        out_shape=jax.ShapeDtypeStruct((b, s, DIFF_COLS), BF16),
        scratch_shapes=_attn_scratch(),
        compiler_params=_params(3),
        name="diff_attn",
    )(lq1, lk1, lq2, lk2, dq, dk, dv, bias, head_gain)


def _moba_kernel(q_ref, k_ref, v_ref, km_ref, bias_ref, o_ref, qs, m_s, l_s, acc_s):
    i = pl.program_id(2)
    t = ATTN_TILE
    lane = lax.broadcasted_iota(jnp.int32, (t, LANES), 1)
    row = lax.broadcasted_iota(jnp.int32, (t, LANES), 0)
    blk = lane - HEAD_DIM
    own = i * (t // MOBA_BLOCK) + row // MOBA_BLOCK
    for hh in range(2):
        qa = q_ref[0, :, hh * LANES:(hh + 1) * LANES]
        g = jnp.dot(qa.astype(F32), km_ref[0, hh], precision=HIGHEST,
                    preferred_element_type=F32)
        g = jnp.where((blk >= 0) & (blk < own), g, -jnp.inf)
        sel = blk == own
        for _ in range(MOBA_TOPK):
            mx = jnp.max(g, axis=1, keepdims=True)
            first = jnp.min(jnp.where(g == mx, lane, 2 * LANES), axis=1, keepdims=True)
            pick = (lane == first) & (mx > -jnp.inf)
            sel = sel | pick
            g = jnp.where(pick, -jnp.inf, g)
        gate = jnp.where(sel, 0.0, NEG).astype(BF16)
        qs[hh] = jnp.where(lane < HEAD_DIM, qa, gate)
    _init_softmax_state(m_s, l_s, acc_s)

    def step(j, which):
        off = pl.multiple_of(j * t, t)
        k = k_ref[0, pl.ds(off, t), :]
        v = v_ref[0, pl.ds(off, t), :]
        for hh in range(2):
            s = _nt_dot(qs[hh], k[:, hh * LANES:(hh + 1) * LANES])
            if which is not None:
                s = s + bias_ref[which, hh]
            _softmax_step(s, v, m_s, l_s, acc_s, hh)

    _causal_sweep(i, step)

    o = jnp.where(lane < HEAD_DIM, _softmax_result(l_s, acc_s, 0), _softmax_result(l_s, acc_s, 1))
    o_ref[0] = o.astype(BF16)


def _moba_attention(mq, mk, mv, km, bias):
    b, s, _ = mq.shape
    t = ATTN_TILE
    return pl.pallas_call(
        _moba_kernel,
        grid=(b, N_MOBA_HEADS // 2, s // t),
        in_specs=[pl.BlockSpec((1, t, 2 * LANES), lambda bi, p, i: (bi, i, p)),
                  pl.BlockSpec((1, s, 2 * LANES), lambda bi, p, i: (bi, 0, p)),
                  pl.BlockSpec((1, s, LANES), lambda bi, p, i: (bi, 0, p)),
                  pl.BlockSpec((1, 2, LANES, LANES), lambda bi, p, i: (bi, p, 0, 0)),
                  pl.BlockSpec((2, 2, t, t), lambda bi, p, i: (0, p, 0, 0))],
        out_specs=pl.BlockSpec((1, t, LANES), lambda bi, p, i: (bi, i, p)),
        out_shape=jax.ShapeDtypeStruct((b, s, MOBA_COLS), BF16),
        scratch_shapes=_attn_scratch(),
        compiler_params=_params(3),
        name="moba_attn",
    )(mq, mk, mv, km, bias)


ROUTER_GROUP_LANE = N_EXPERTS
ROUTE_E1, ROUTE_E2, ROUTE_R1, ROUTE_R2, ROUTE_W1, ROUTE_W2 = range(6)


def _mid_kernel(yd_ref, ym_ref, x_ref, wo_ref, pg_ref, g1_ref, fg_ref, sc2_ref, sh2_ref,
                wr_ref, br_ref, x1_ref, h2_ref, route_ref, tot_ref, run):
    first = (pl.program_id(0) == 0) & (pl.program_id(1) == 0)

    @pl.when(first)
    def _():
        run[...] = jnp.zeros(run.shape, F32)

    y = (jnp.dot(yd_ref[0], wo_ref[0:DIFF_COLS], preferred_element_type=F32)
         + jnp.dot(ym_ref[0], wo_ref[DIFF_COLS:D_MODEL], preferred_element_type=F32))
    x1 = x_ref[0] + g1_ref[0] * (_rms(y) * pg_ref[...])
    x1_ref[0] = x1
    h2 = _rms(x1) * fg_ref[...] * (1.0 + sc2_ref[0]) + sh2_ref[0]
    h2_ref[0] = h2

    logits = jnp.dot(h2, wr_ref[...], precision=HIGHEST, preferred_element_type=F32) + br_ref[...]
    lane = lax.broadcasted_iota(jnp.int32, logits.shape, 1)
    is_group = (lane >= ROUTER_GROUP_LANE) & (lane < ROUTER_GROUP_LANE + N_GROUPS)
    gl = jnp.where(is_group, logits, -jnp.inf)
    gmax = jnp.max(gl, axis=1, keepdims=True)
    g_idx = jnp.min(jnp.where(gl == gmax, lane - ROUTER_GROUP_LANE, N_GROUPS), axis=1, keepdims=True)
    g_w = 1.0 / jnp.sum(jnp.exp(gl - gmax), axis=1, keepdims=True)
    in_group = (lane < N_EXPERTS) & (lane // EXPERTS_PER_GROUP == g_idx)
    el = jnp.where(in_group, logits, -jnp.inf)
    m1 = jnp.max(el, axis=1, keepdims=True)
    i1 = jnp.min(jnp.where(el == m1, lane, LANES), axis=1, keepdims=True)
    el2 = jnp.where(lane == i1, -jnp.inf, el)
    m2 = jnp.max(el2, axis=1, keepdims=True)
    i2 = jnp.min(jnp.where(el2 == m2, lane, LANES), axis=1, keepdims=True)
    e2 = jnp.exp(m2 - m1)
    w1 = g_w / (1.0 + e2)
    w2 = g_w * e2 / (1.0 + e2)

    t = MID_TILE
    used = jnp.where((lane == i1) | (lane == i2), 1.0, 0.0)
    earlier = jnp.where(lax.broadcasted_iota(jnp.int32, (t, t), 1)
                        < lax.broadcasted_iota(jnp.int32, (t, t), 0), 1.0, 0.0).astype(BF16)
    rank = jnp.dot(earlier, used.astype(BF16), preferred_element_type=F32) + run[...]
    r1 = jnp.sum(jnp.where(lane == i1, rank, 0.0), axis=1, keepdims=True)
    r2 = jnp.sum(jnp.where(lane == i2, rank, 0.0), axis=1, keepdims=True)
    record = jnp.zeros(logits.shape, F32)
    for field, val in ((ROUTE_E1, i1.astype(F32)), (ROUTE_E2, i2.astype(F32)), (ROUTE_R1, r1),
                       (ROUTE_R2, r2), (ROUTE_W1, w1), (ROUTE_W2, w2)):
        record = jnp.where(lane == field, val, record)
    route_ref[0] = record
    run[...] += jnp.sum(used, axis=0, keepdims=True)
    tot_ref[...] = run[...]


def _mid(yd, ym, x, w_out, post_mix_gain, g1, pre_ffn_gain, sc2, sh2, w_router, b_router):
    b, s, d = x.shape
    t = MID_TILE
    tok = lambda width: pl.BlockSpec((1, t, width), lambda bi, i: (bi, i, 0))
    vec = pl.BlockSpec((1, 1, d), lambda bi, i: (bi, 0, 0))
    row = lambda width: pl.BlockSpec((1, width), lambda bi, i: (0, 0))
    return pl.pallas_call(
        _mid_kernel,
        grid=(b, s // t),
        in_specs=[tok(DIFF_COLS), tok(MOBA_COLS), tok(d),
                  pl.BlockSpec((d, d), lambda bi, i: (0, 0)),
                  row(d), vec, row(d), vec, vec,
                  pl.BlockSpec((d, LANES), lambda bi, i: (0, 0)), row(LANES)],
        out_specs=[tok(d), tok(d), tok(LANES), row(LANES)],
        out_shape=[jax.ShapeDtypeStruct((b, s, d), F32),
                   jax.ShapeDtypeStruct((b, s, d), F32),
                   jax.ShapeDtypeStruct((b, s, LANES), F32),
                   jax.ShapeDtypeStruct((1, LANES), F32)],
        scratch_shapes=[pltpu.VMEM((1, LANES), F32)],
        compiler_params=_params(2),
        name="out_proj_router",
    )(yd, ym, x, w_out, post_mix_gain, g1, pre_ffn_gain, sc2, sh2, w_router, b_router)


def _row_copy(src, src_row, dst, dst_row, sem):
    return pltpu.make_async_copy(src.at[pl.ds(src_row, 1), :], dst.at[pl.ds(dst_row, 1), :], sem)


def _dispatch_kernel(pos_ref, h_ref, xs_in_ref, xs_ref, sem):
    del xs_in_ref

    def issue(t, carry):
        for k in range(2):
            _row_copy(h_ref, t, xs_ref, pos_ref[0, 0, 2 * t + k], sem).start()
        return carry

    lax.fori_loop(0, DISPATCH_TILE, issue, 0)

    def drain(t, carry):
        for k in range(2):
            _row_copy(h_ref, t, xs_ref, pos_ref[0, 0, 2 * t + k], sem).wait()
        return carry

    lax.fori_loop(0, DISPATCH_TILE, drain, 0)


def _dispatch(pos, h2, xs_zero):
    n, d = h2.shape
    t = DISPATCH_TILE
    return pl.pallas_call(
        _dispatch_kernel,
        grid=(n // t,),
        in_specs=[pl.BlockSpec((1, 1, 2 * t), lambda i: (i, 0, 0), memory_space=pltpu.SMEM),
                  pl.BlockSpec((t, d), lambda i: (i, 0)),
                  pl.BlockSpec(memory_space=pl.ANY)],
        out_specs=pl.BlockSpec(memory_space=pl.ANY),
        out_shape=jax.ShapeDtypeStruct(xs_zero.shape, F32),
        scratch_shapes=[pltpu.SemaphoreType.DMA(())],
        input_output_aliases={2: 0},
        compiler_params=_params(1),
        name="moe_dispatch",
    )(pos.reshape(n // t, 1, 2 * t), h2, xs_zero)


def _expert_kernel(te_ref, x_ref, wg_ref, wu_ref, wd_ref, y_ref):
    del te_ref
    x = x_ref[...].astype(BF16)
    a = jnp.dot(x, wg_ref[0], preferred_element_type=F32)
    u = jnp.dot(x, wu_ref[0], preferred_element_type=F32)
    hid = (a * jax.nn.sigmoid(a)) * u
    y_ref[...] = jnp.dot(hid.astype(BF16), wd_ref[0], preferred_element_type=F32)


def _experts(tile_expert, xs, w_gate, w_up, w_down):
    rows, d = xs.shape
    t = EXPERT_TILE
    return pl.pallas_call(
        _expert_kernel,
        grid_spec=pltpu.PrefetchScalarGridSpec(
            num_scalar_prefetch=1,
            grid=(rows // t,),
            in_specs=[pl.BlockSpec((t, d), lambda j, te: (j, 0)),
                      pl.BlockSpec((1, d, EXPERT_FF), lambda j, te: (te[j], 0, 0)),
                      pl.BlockSpec((1, d, EXPERT_FF), lambda j, te: (te[j], 0, 0)),
                      pl.BlockSpec((1, EXPERT_FF, d), lambda j, te: (te[j], 0, 0))],
            out_specs=pl.BlockSpec((t, d), lambda j, te: (j, 0))),
        out_shape=jax.ShapeDtypeStruct((rows, d), F32),
        compiler_params=_params(1),
        name="routed_experts",
    )(tile_expert, xs, w_gate, w_up, w_down)


def _combine_kernel(pos_ref, y_ref, route_ref, x1_ref, g2_ref, pg_ref, o_ref, buf, sem):
    t = COMBINE_TILE

    def issue(r, carry):
        for k in range(2):
            _row_copy(y_ref, pos_ref[0, 0, 2 * r + k], buf.at[k], r, sem).start()
        return carry

    lax.fori_loop(0, t, issue, 0)

    def drain(r, carry):
        for k in range(2):
            _row_copy(y_ref, pos_ref[0, 0, 2 * r + k], buf.at[k], r, sem).wait()
        return carry

    lax.fori_loop(0, t, drain, 0)

    route = route_ref[0]
    w1 = route[:, ROUTE_W1:ROUTE_W1 + 1]
    w2 = route[:, ROUTE_W2:ROUTE_W2 + 1]
    y = w1 * buf[0] + w2 * buf[1]
    o_ref[0] = x1_ref[0] + g2_ref[0] * (_rms(y) * pg_ref[...])


def _combine(pos, ys, route, x1, g2, post_ffn_gain):
    b, s, d = x1.shape
    t = COMBINE_TILE
    nt = s // t
    tok = lambda width: pl.BlockSpec((1, t, width), lambda bi, i: (bi, i, 0))
    return pl.pallas_call(
        _combine_kernel,
        grid=(b, nt),
        in_specs=[pl.BlockSpec((1, 1, 2 * t), lambda bi, i: (bi * nt + i, 0, 0),
                               memory_space=pltpu.SMEM),
                  pl.BlockSpec(memory_space=pl.ANY),
                  tok(LANES), tok(d),
                  pl.BlockSpec((1, 1, d), lambda bi, i: (bi, 0, 0)),
                  pl.BlockSpec((1, d), lambda bi, i: (0, 0))],
        out_specs=tok(d),
        out_shape=jax.ShapeDtypeStruct((b, s, d), F32),
        scratch_shapes=[pltpu.VMEM((2, t, d), F32), pltpu.SemaphoreType.DMA(())],
        compiler_params=_params(2),
        name="moe_combine",
    )(pos.reshape(b * nt, 1, 2 * t), ys, route, x1, g2, post_ffn_gain)


def _routing_tables(route, totals, n_rows):
    t = EXPERT_TILE
    tot = totals[0, :N_EXPERTS].astype(jnp.int32)
    padded = (tot + t - 1) // t * t
    ends = jnp.cumsum(padded)
    base = ends - padded
    rec = route.reshape(-1, LANES)
    e = rec[:, ROUTE_E1:ROUTE_E2 + 1].astype(jnp.int32)
    r = rec[:, ROUTE_R1:ROUTE_R2 + 1].astype(jnp.int32)
    pos = base[e] + r
    tile_start = jnp.arange(n_rows // t, dtype=jnp.int32) * t
    tile_expert = jnp.minimum(jnp.sum(tile_start[:, None] >= ends[None, :], axis=1),
                              N_EXPERTS - 1).astype(jnp.int32)
    return pos.reshape(-1), tile_expert


def _augment_w_in(w_in):
    d = w_in.shape[0]
    scale = HEAD_DIM ** -0.5 * LOG2E
    dq, dk, dv, mq, mk, mv = jnp.split(w_in, [512, 1024, 1536, 2048, 2560], axis=1)

    def pad_heads(w):
        w = w.reshape(d, N_MOBA_HEADS, HEAD_DIM)
        return jnp.pad(w, ((0, 0), (0, 0), (0, LANES - HEAD_DIM))).reshape(d, N_MOBA_HEADS * LANES)

    return jnp.concatenate([dq * scale, dk, dv, pad_heads(mq * scale), pad_heads(mk), mv],
                           axis=1).astype(BF16)


def kernel(x, c, w_ada, b_ada, pre_mix_gain, post_mix_gain, pre_ffn_gain, post_ffn_gain, w_in, lambda_q1, lambda_k1, lambda_q2, lambda_k2, diff_head_gain, w_out, rel_bias, w_group, b_group, w_expert, b_expert, w_gate, w_up, w_down):
    b, s, d = x.shape
    depth = w_in.shape[0]
    tab_t = rel_bias.T
    bias_diff = _bias_tiles(tab_t, ATTN_TILE, 0, N_DIFF_HEADS)
    bias_moba = _bias_tiles(tab_t, ATTN_TILE, N_DIFF_HEADS, N_MOBA_HEADS)
    c_pad = jnp.pad(c, ((0, 8 - b), (0, 0)))
    for l in range(depth):
        lambda_init = 0.8 - 0.6 * math.exp(-0.3 * l)
        mod = _ada(c_pad, w_ada[l], b_ada[l][None])[:b]
        sh1, sc1, g1, sh2, sc2, g2 = [m[:, None, :] for m in jnp.split(mod, 6, axis=-1)]

        dq, dk, dv, mq, mk, mv, kmean = _inproj(x, pre_mix_gain[l][None], sc1, sh1,
                                                _augment_w_in(w_in[l]))
        y_diff = _diff_attention(dq, dk, dv, bias_diff, lambda_q1[l][None], lambda_k1[l][None],
                                 lambda_q2[l][None], lambda_k2[l][None],
                                 diff_head_gain[l][None], lambda_init)
        nkb = s // MOBA_BLOCK
        km = kmean.reshape(b, nkb, N_MOBA_HEADS, LANES)[..., :HEAD_DIM]
        km = jnp.pad(km.transpose(0, 2, 3, 1),
                     ((0, 0), (0, 0), (0, LANES - HEAD_DIM), (HEAD_DIM, LANES - HEAD_DIM - nkb)))
        y_moba = _moba_attention(mq, mk, mv, km, bias_moba)

        w_router = jnp.pad(jnp.concatenate([w_expert[l], w_group[l]], axis=1),
                           ((0, 0), (0, LANES - N_EXPERTS - N_GROUPS)))
        b_router = jnp.pad(jnp.concatenate([b_expert[l], b_group[l]]),
                           (0, LANES - N_EXPERTS - N_GROUPS))[None]
        x1, h2, route, totals = _mid(
            y_diff, y_moba, x, w_out[l].astype(BF16), post_mix_gain[l][None], g1,
            pre_ffn_gain[l][None], sc2, sh2, w_router, b_router)
        n_rows = 2 * b * s + N_EXPERTS * EXPERT_TILE
        pos, tile_expert = _routing_tables(route, totals, n_rows)
        xs = _dispatch(pos, h2.reshape(b * s, d), jnp.zeros((n_rows, d), F32))
        ys = _experts(tile_expert, xs, w_gate[l].astype(BF16), w_up[l].astype(BF16),
                      w_down[l].astype(BF16))
        x = _combine(pos, ys, route, x1, g2, post_ffn_gain[l][None])
    return x
```

```python
import functools
import math

import jax
import jax.numpy as jnp
from jax import lax
from jax.experimental import pallas as pl
from jax.experimental.pallas import tpu as pltpu

F32 = jnp.float32
BF16 = jnp.bfloat16
HIGHEST = lax.Precision.HIGHEST

D_MODEL = 1024
HEAD_DIM = 64
N_DIFF_HEADS = 4
N_MOBA_HEADS = 8
DIFF_COLS = 512
MOBA_COLS = 512
MOBA_BLOCK = 256
MOBA_TOPK = 3
N_BUCKETS = 32
MAX_EXACT = N_BUCKETS // 2
MAX_DISTANCE = 128
N_GROUPS = 4
EXPERTS_PER_GROUP = 8
N_EXPERTS = 32
EXPERT_FF = 512
EPS = 1e-6
NEG = -1e30
LANES = 128
LOG2E = math.log2(math.e)
ONES_ROWS = 16
DIFF_V_ROWS = 2 * HEAD_DIM + ONES_ROWS
MOBA_V_ROWS = HEAD_DIM + ONES_ROWS

IN_TILE = 512
ATTN_TILE = 512
MID_TILE = 512
DISPATCH_TILE = 512
EXPERT_TILE = 256
COMBINE_TILE = 256
VMEM_LIMIT = 56 * 1024 * 1024


def _params(n_axes, vmem=VMEM_LIMIT):
    return pltpu.CompilerParams(dimension_semantics=("arbitrary",) * n_axes,
                                vmem_limit_bytes=vmem)


def _rms(v):
    return v * lax.rsqrt(jnp.mean(v * v, axis=-1, keepdims=True) + EPS)


def _ada_kernel(c_ref, w_ref, b_ref, o_ref):
    c = c_ref[...]
    s = c * jax.nn.sigmoid(c)
    o_ref[...] = jnp.dot(s, w_ref[...], precision=HIGHEST,
                         preferred_element_type=F32) + b_ref[...]


def _ada(c_pad, w_ada, b_ada):
    n = w_ada.shape[1]
    return pl.pallas_call(
        _ada_kernel,
        grid=(n // D_MODEL,),
        in_specs=[pl.BlockSpec((8, D_MODEL), lambda j: (0, 0)),
                  pl.BlockSpec((D_MODEL, D_MODEL), lambda j: (0, j)),
                  pl.BlockSpec((1, D_MODEL), lambda j: (0, j))],
        out_specs=pl.BlockSpec((8, D_MODEL), lambda j: (0, j)),
        out_shape=jax.ShapeDtypeStruct((8, n), F32),
        compiler_params=_params(1),
        name="ada_mod",
    )(c_pad, w_ada, b_ada)


def _bias_kernel(tab_ref, o_ref, *, tile, head0):
    w = pl.program_id(0)
    h = pl.program_id(1) + head0
    r = lax.broadcasted_iota(jnp.int32, (tile, tile), 0)
    c = lax.broadcasted_iota(jnp.int32, (tile, tile), 1)
    d = c - r + w * tile
    n = jnp.maximum(d, 0)
    nf = jnp.maximum(n, 1).astype(F32)
    large = MAX_EXACT + (jnp.log(nf / MAX_EXACT) / math.log(MAX_DISTANCE / MAX_EXACT)
                         * (N_BUCKETS - MAX_EXACT)).astype(jnp.int32)
    large = jnp.minimum(large, N_BUCKETS - 1)
    bucket = jnp.where(n < MAX_EXACT, n, large)
    far = tab_ref[h, N_BUCKETS - 1]
    val = jnp.zeros((tile, tile), F32)
    for b in range(N_BUCKETS - 1):
        val = jnp.where(bucket == b, (tab_ref[h, b] - far) * LOG2E, val)
    o_ref[0, 0] = jnp.where(d >= 0, val, NEG)


def _bias_tiles(tab_t, tile, head0, n_heads):
    return pl.pallas_call(
        functools.partial(_bias_kernel, tile=tile, head0=head0),
        grid=(2, n_heads),
        in_specs=[pl.BlockSpec(memory_space=pltpu.SMEM)],
        out_specs=pl.BlockSpec((1, 1, tile, tile), lambda w, h: (w, h, 0, 0)),
        out_shape=jax.ShapeDtypeStruct((2, n_heads, tile, tile), F32),
        compiler_params=_params(2),
        name="bias_tiles",
    )(tab_t)


W_DQ, W_DK, W_DV = 0, 512, 1024
W_MQ, W_MK, W_MV = 1536, 2560, 3584
W_END = 4096


def _inproj_kernel(x_ref, gain_ref, sc_ref, sh_ref, w_ref,
                   dq_ref, dk_ref, dv_ref, mq_ref, mk_ref, mv_ref, km_ref):
    i = pl.program_id(1)
    h = _rms(x_ref[0]) * gain_ref[...]
    h = h * (1.0 + sc_ref[0]) + sh_ref[0]
    hb = h.astype(BF16)

    def proj(lo, hi):
        return jnp.dot(hb, w_ref[:, lo:hi], preferred_element_type=F32)

    ones = jnp.ones((ONES_ROWS, IN_TILE), F32)

    def with_ones(vt, rows):
        parts = []
        for r0 in range(0, vt.shape[0], rows):
            parts += [vt[r0:r0 + rows], ones]
        return jnp.concatenate(parts, axis=0).astype(BF16)

    dq_ref[0] = proj(W_DQ, W_DK).T.astype(BF16)
    dk_ref[0] = proj(W_DK, W_DV).astype(BF16)
    dv_ref[0] = with_ones(proj(W_DV, W_MQ).T, 2 * HEAD_DIM)
    mq_ref[0] = proj(W_MQ, W_MK).T.astype(BF16)
    mv_ref[0] = with_ones(proj(W_MV, W_END).T, HEAD_DIM)
    mk = proj(W_MK, W_MV)
    for r in range(IN_TILE // MOBA_BLOCK):
        km_ref[0, r] = jnp.mean(mk[r * MOBA_BLOCK:(r + 1) * MOBA_BLOCK], axis=0, keepdims=True)
    row = lax.broadcasted_iota(jnp.int32, mk.shape, 0)
    lane = lax.broadcasted_iota(jnp.int32, mk.shape, 1)
    blk = (i * IN_TILE + row) // MOBA_BLOCK
    onehot = (lane % LANES) - HEAD_DIM == blk
    mk_ref[0] = jnp.where(onehot, 1.0, mk).astype(BF16)


def _inproj(x, gain, sc1, sh1, w_aug):
    b, s, d = x.shape
    nkb = s // MOBA_BLOCK
    tok = lambda width: pl.BlockSpec((1, IN_TILE, width), lambda bi, i: (bi, i, 0))
    tok_t = lambda rows: pl.BlockSpec((1, rows, IN_TILE), lambda bi, i: (bi, 0, i))
    vec = pl.BlockSpec((1, 1, d), lambda bi, i: (bi, 0, 0))
    shp = lambda width: jax.ShapeDtypeStruct((b, s, width), BF16)
    shp_t = lambda rows: jax.ShapeDtypeStruct((b, rows, s), BF16)
    dv_rows = N_DIFF_HEADS * DIFF_V_ROWS
    mv_rows = N_MOBA_HEADS * MOBA_V_ROWS
    return pl.pallas_call(
        _inproj_kernel,
        grid=(b, s // IN_TILE),
        in_specs=[tok(d), pl.BlockSpec((1, d), lambda bi, i: (0, 0)), vec, vec,
                  pl.BlockSpec((d, W_END), lambda bi, i: (0, 0))],
        out_specs=[tok_t(512), tok(512), tok_t(dv_rows), tok_t(1024), tok(1024), tok_t(mv_rows),
                   pl.BlockSpec((1, IN_TILE // MOBA_BLOCK, 1, 1024), lambda bi, i: (bi, i, 0, 0))],
        out_shape=[shp_t(512), shp(512), shp_t(dv_rows), shp_t(1024), shp(1024), shp_t(mv_rows),
                   jax.ShapeDtypeStruct((b, nkb, 1, 1024), F32)],
        compiler_params=_params(2),
        name="in_proj",
    )(x, gain, sc1, sh1, w_aug)


def _softmax_step(st, vt, m_ref, acc_ref, idx):
    m_prev = m_ref[idx]
    m_next = jnp.maximum(m_prev, jnp.max(st, axis=0, keepdims=True))
    alpha = jnp.exp2(m_prev - m_next)
    p = jnp.exp2(st - m_next).astype(BF16)
    acc_ref[idx] = alpha * acc_ref[idx] + jnp.dot(vt, p, preferred_element_type=F32)
    m_ref[idx] = m_next


def _softmax_result(acc_ref, idx, rows):
    acc = acc_ref[idx]
    return acc[:rows] / acc[rows:rows + 1]


def _init_softmax_state(m_s, acc_s):
    m_s[...] = jnp.full(m_s.shape, -jnp.inf, F32)
    acc_s[...] = jnp.zeros(acc_s.shape, F32)


def _causal_sweep(i, scores, consume, s_scr):
    def direct(j, which):
        for m in range(2):
            consume(j, m, scores(j, m, which))

    direct(i, 0)

    @pl.when(i > 0)
    def _():
        direct(i - 1, 1)

    n_far = jnp.maximum(i - 1, 0)

    def produce(j, slot):
        for m in range(2):
            s_scr[slot, m] = scores(j, m, None)

    def staged(j, slot):
        for m in range(2):
            consume(j, m, s_scr[slot, m])

    @pl.when(n_far > 0)
    def _():
        produce(0, 0)

    def far_pair(jj, carry):
        produce(2 * jj + 1, 1)
        staged(2 * jj, 0)
        produce(jnp.minimum(2 * jj + 2, n_far - 1), 0)
        staged(2 * jj + 1, 1)
        return carry

    lax.fori_loop(0, n_far // 2, far_pair, 0)

    @pl.when(n_far % 2 == 1)
    def _():
        staged(n_far - 1, 0)


def _attn_scratch(v_rows):
    t = ATTN_TILE
    return [pltpu.VMEM((2, LANES, t), BF16), pltpu.VMEM((2, 1, t), F32),
            pltpu.VMEM((2, v_rows, t), F32), pltpu.VMEM((2, 2, t, t), F32)]


def _diff_kernel(lq1_ref, lk1_ref, lq2_ref, lk2_ref, q_ref, k_ref, v_ref, bias_ref, gain_ref,
                 o_ref, qs, m_s, acc_s, s_scr, *, lambda_init):
    i = pl.program_id(2)
    t = ATTN_TILE
    q = q_ref[0]
    row = lax.broadcasted_iota(jnp.int32, q.shape, 0)
    zero = jnp.zeros_like(q)
    qs[0] = jnp.where(row < HEAD_DIM, q, zero)
    qs[1] = jnp.where(row >= HEAD_DIM, q, zero)
    _init_softmax_state(m_s, acc_s)

    def scores(j, m, which):
        k = k_ref[0, pl.ds(pl.multiple_of(j * t, t), t), :]
        s = jnp.dot(k, qs[m], preferred_element_type=F32)
        return s if which is None else s + bias_ref[which, 0]

    def consume(j, m, st):
        vt = v_ref[0, :, pl.ds(pl.multiple_of(j * t, t), t)]
        _softmax_step(st, vt, m_s, acc_s, m)

    _causal_sweep(i, scores, consume, s_scr)

    lam = (jnp.exp(jnp.sum(lq1_ref[...] * lk1_ref[...], axis=0, keepdims=True))
           - jnp.exp(jnp.sum(lq2_ref[...] * lk2_ref[...], axis=0, keepdims=True)) + lambda_init)
    dv = 2 * HEAD_DIM
    a = _softmax_result(acc_s, 0, dv) - lam * _softmax_result(acc_s, 1, dv)
    y = a * lax.rsqrt(jnp.mean(a * a, axis=0, keepdims=True) + EPS)
    o_ref[0] = (y * gain_ref[...] * (1.0 - lambda_init)).astype(BF16)


def _diff_attention(dq_t, dk, dv_t, bias, lq1, lk1, lq2, lk2, head_gain, lambda_init):
    b, _, s = dq_t.shape
    t = ATTN_TILE
    col = lambda rows: pl.BlockSpec((rows, 1), lambda bi, h, i: (0, 0))
    return pl.pallas_call(
        functools.partial(_diff_kernel, lambda_init=lambda_init),
        grid=(b, N_DIFF_HEADS, s // t),
        in_specs=[col(HEAD_DIM), col(HEAD_DIM), col(HEAD_DIM), col(HEAD_DIM),
                  pl.BlockSpec((1, LANES, t), lambda bi, h, i: (bi, h, i)),
                  pl.BlockSpec((1, s, LANES), lambda bi, h, i: (bi, 0, h)),
                  pl.BlockSpec((1, DIFF_V_ROWS, s), lambda bi, h, i: (bi, h, 0)),
                  pl.BlockSpec((2, 1, t, t), lambda bi, h, i: (0, h, 0, 0)),
                  col(2 * HEAD_DIM)],
        out_specs=pl.BlockSpec((1, LANES, t), lambda bi, h, i: (bi, h, i)),
        out_shape=jax.ShapeDtypeStruct((b, DIFF_COLS, s), BF16),
        scratch_shapes=_attn_scratch(DIFF_V_ROWS),
        compiler_params=_params(3),
        name="diff_attn",
    )(lq1, lk1, lq2, lk2, dq_t, dk, dv_t, bias, head_gain)


def _moba_kernel(q_ref, k_ref, v_ref, km_ref, bias_ref, o_ref, qs, m_s, acc_s, s_scr):
    i = pl.program_id(2)
    t = ATTN_TILE
    row = lax.broadcasted_iota(jnp.int32, (LANES, t), 0)
    lane = lax.broadcasted_iota(jnp.int32, (LANES, t), 1)
    blk = row - HEAD_DIM
    own = i * (t // MOBA_BLOCK) + lane // MOBA_BLOCK
    for hh in range(2):
        qa = q_ref[0, hh * LANES:(hh + 1) * LANES, :]
        g = jnp.dot(km_ref[0, hh], qa.astype(F32), precision=HIGHEST,
                    preferred_element_type=F32)
        g = jnp.where((blk >= 0) & (blk < own), g, -jnp.inf)
        sel = blk == own
        for _ in range(MOBA_TOPK):
            mx = jnp.max(g, axis=0, keepdims=True)
            first = jnp.min(jnp.where(g == mx, row, 2 * LANES), axis=0, keepdims=True)
            pick = (row == first) & (mx > -jnp.inf)
            sel = sel | pick
            g = jnp.where(pick, -jnp.inf, g)
        gate = jnp.where(sel, 0.0, NEG).astype(BF16)
        qs[hh] = jnp.where(row < HEAD_DIM, qa, gate)
    _init_softmax_state(m_s, acc_s)

    def scores(j, hh, which):
        k = k_ref[0, pl.ds(pl.multiple_of(j * t, t), t), hh * LANES:(hh + 1) * LANES]
        s = jnp.dot(k, qs[hh], preferred_element_type=F32)
        return s if which is None else s + bias_ref[which, hh]

    def consume(j, hh, st):
        vt = v_ref[0, hh * MOBA_V_ROWS:(hh + 1) * MOBA_V_ROWS, pl.ds(pl.multiple_of(j * t, t), t)]
        _softmax_step(st, vt, m_s, acc_s, hh)

    _causal_sweep(i, scores, consume, s_scr)

    o = jnp.concatenate([_softmax_result(acc_s, 0, HEAD_DIM), _softmax_result(acc_s, 1, HEAD_DIM)],
                        axis=0)
    o_ref[0] = o.astype(BF16)


def _moba_attention(mq_t, mk, mv_t, km, bias):
    b, _, s = mq_t.shape
    t = ATTN_TILE
    return pl.pallas_call(
        _moba_kernel,
        grid=(b, N_MOBA_HEADS // 2, s // t),
        in_specs=[pl.BlockSpec((1, 2 * LANES, t), lambda bi, p, i: (bi, p, i)),
                  pl.BlockSpec((1, s, 2 * LANES), lambda bi, p, i: (bi, 0, p)),
                  pl.BlockSpec((1, 2 * MOBA_V_ROWS, s), lambda bi, p, i: (bi, p, 0)),
                  pl.BlockSpec((1, 2, LANES, LANES), lambda bi, p, i: (bi, p, 0, 0)),
                  pl.BlockSpec((2, 2, t, t), lambda bi, p, i: (0, p, 0, 0))],
        out_specs=pl.BlockSpec((1, LANES, t), lambda bi, p, i: (bi, p, i)),
        out_shape=jax.ShapeDtypeStruct((b, MOBA_COLS, s), BF16),
        scratch_shapes=_attn_scratch(MOBA_V_ROWS),
        compiler_params=_params(3),
        name="moba_attn",
    )(mq_t, mk, mv_t, km, bias)


ROUTER_GROUP_LANE = N_EXPERTS
ROUTE_E1, ROUTE_E2, ROUTE_R1, ROUTE_R2, ROUTE_W1, ROUTE_W2 = range(6)


def _mid_kernel(yd_ref, ym_ref, x_ref, wo_ref, pg_ref, g1_ref, fg_ref, sc2_ref, sh2_ref,
                wr_ref, br_ref, x1_ref, h2_ref, route_ref, tot_ref, run):
    first = (pl.program_id(0) == 0) & (pl.program_id(1) == 0)

    @pl.when(first)
    def _():
        run[...] = jnp.zeros(run.shape, F32)

    tn = (((0,), (0,)), ((), ()))
    y = (lax.dot_general(yd_ref[0], wo_ref[0:DIFF_COLS], tn, preferred_element_type=F32)
         + lax.dot_general(ym_ref[0], wo_ref[DIFF_COLS:D_MODEL], tn, preferred_element_type=F32))
    x1 = x_ref[0] + g1_ref[0] * (_rms(y) * pg_ref[...])
    x1_ref[0] = x1
    h2 = _rms(x1) * fg_ref[...] * (1.0 + sc2_ref[0]) + sh2_ref[0]
    h2_ref[0] = h2

    logits = jnp.dot(h2, wr_ref[...], precision=HIGHEST, preferred_element_type=F32) + br_ref[...]
    lane = lax.broadcasted_iota(jnp.int32, logits.shape, 1)
    is_group = (lane >= ROUTER_GROUP_LANE) & (lane < ROUTER_GROUP_LANE + N_GROUPS)
    gl = jnp.where(is_group, logits, -jnp.inf)
    gmax = jnp.max(gl, axis=1, keepdims=True)
    g_idx = jnp.min(jnp.where(gl == gmax, lane - ROUTER_GROUP_LANE, N_GROUPS), axis=1, keepdims=True)
    g_w = 1.0 / jnp.sum(jnp.exp(gl - gmax), axis=1, keepdims=True)
    in_group = (lane < N_EXPERTS) & (lane // EXPERTS_PER_GROUP == g_idx)
    el = jnp.where(in_group, logits, -jnp.inf)
    m1 = jnp.max(el, axis=1, keepdims=True)
    i1 = jnp.min(jnp.where(el == m1, lane, LANES), axis=1, keepdims=True)
    el2 = jnp.where(lane == i1, -jnp.inf, el)
    m2 = jnp.max(el2, axis=1, keepdims=True)
    i2 = jnp.min(jnp.where(el2 == m2, lane, LANES), axis=1, keepdims=True)
    e2 = jnp.exp(m2 - m1)
    w1 = g_w / (1.0 + e2)
    w2 = g_w * e2 / (1.0 + e2)

    t = MID_TILE
    used = jnp.where((lane == i1) | (lane == i2), 1.0, 0.0)
    earlier = jnp.where(lax.broadcasted_iota(jnp.int32, (t, t), 1)
                        < lax.broadcasted_iota(jnp.int32, (t, t), 0), 1.0, 0.0).astype(BF16)
    rank = jnp.dot(earlier, used.astype(BF16), preferred_element_type=F32) + run[...]
    r1 = jnp.sum(jnp.where(lane == i1, rank, 0.0), axis=1, keepdims=True)
    r2 = jnp.sum(jnp.where(lane == i2, rank, 0.0), axis=1, keepdims=True)
    record = jnp.zeros(logits.shape, F32)
    for field, val in ((ROUTE_E1, i1.astype(F32)), (ROUTE_E2, i2.astype(F32)), (ROUTE_R1, r1),
                       (ROUTE_R2, r2), (ROUTE_W1, w1), (ROUTE_W2, w2)):
        record = jnp.where(lane == field, val, record)
    route_ref[0] = record
    run[...] += jnp.sum(used, axis=0, keepdims=True)
    tot_ref[...] = run[...]


def _mid(yd, ym, x, w_out, post_mix_gain, g1, pre_ffn_gain, sc2, sh2, w_router, b_router):
    b, s, d = x.shape
    t = MID_TILE
    tok = lambda width: pl.BlockSpec((1, t, width), lambda bi, i: (bi, i, 0))
    vec = pl.BlockSpec((1, 1, d), lambda bi, i: (bi, 0, 0))
    row = lambda width: pl.BlockSpec((1, width), lambda bi, i: (0, 0))
    return pl.pallas_call(
        _mid_kernel,
        grid=(b, s // t),
        in_specs=[pl.BlockSpec((1, DIFF_COLS, t), lambda bi, i: (bi, 0, i)),
                  pl.BlockSpec((1, MOBA_COLS, t), lambda bi, i: (bi, 0, i)), tok(d),
                  pl.BlockSpec((d, d), lambda bi, i: (0, 0)),
                  row(d), vec, row(d), vec, vec,
                  pl.BlockSpec((d, LANES), lambda bi, i: (0, 0)), row(LANES)],
        out_specs=[tok(d), tok(d), tok(LANES), row(LANES)],
        out_shape=[jax.ShapeDtypeStruct((b, s, d), F32),
                   jax.ShapeDtypeStruct((b, s, d), F32),
                   jax.ShapeDtypeStruct((b, s, LANES), F32),
                   jax.ShapeDtypeStruct((1, LANES), F32)],
        scratch_shapes=[pltpu.VMEM((1, LANES), F32)],
        compiler_params=_params(2),
        name="out_proj_router",
    )(yd, ym, x, w_out, post_mix_gain, g1, pre_ffn_gain, sc2, sh2, w_router, b_router)


def _row_copy(src, src_row, dst, dst_row, sem):
    return pltpu.make_async_copy(src.at[pl.ds(src_row, 1), :], dst.at[pl.ds(dst_row, 1), :], sem)


def _dispatch_kernel(pos_ref, h_ref, xs_in_ref, xs_ref, sem):
    del xs_in_ref

    def issue(t, carry):
        for k in range(2):
            _row_copy(h_ref, t, xs_ref, pos_ref[0, 0, 2 * t + k], sem).start()
        return carry

    lax.fori_loop(0, DISPATCH_TILE, issue, 0)

    def drain(t, carry):
        for k in range(2):
            _row_copy(h_ref, t, xs_ref, pos_ref[0, 0, 2 * t + k], sem).wait()
        return carry

    lax.fori_loop(0, DISPATCH_TILE, drain, 0)


def _dispatch(pos, h2, xs_zero):
    n, d = h2.shape
    t = DISPATCH_TILE
    return pl.pallas_call(
        _dispatch_kernel,
        grid=(n // t,),
        in_specs=[pl.BlockSpec((1, 1, 2 * t), lambda i: (i, 0, 0), memory_space=pltpu.SMEM),
                  pl.BlockSpec((t, d), lambda i: (i, 0)),
                  pl.BlockSpec(memory_space=pl.ANY)],
        out_specs=pl.BlockSpec(memory_space=pl.ANY),
        out_shape=jax.ShapeDtypeStruct(xs_zero.shape, F32),
        scratch_shapes=[pltpu.SemaphoreType.DMA(())],
        input_output_aliases={2: 0},
        compiler_params=_params(1),
        name="moe_dispatch",
    )(pos.reshape(n // t, 1, 2 * t), h2, xs_zero)


def _expert_kernel(te_ref, x_ref, wg_ref, wu_ref, wd_ref, y_ref):
    del te_ref
    x = x_ref[...].astype(BF16)
    a = jnp.dot(x, wg_ref[0], preferred_element_type=F32)
    u = jnp.dot(x, wu_ref[0], preferred_element_type=F32)
    hid = (a * jax.nn.sigmoid(a)) * u
    y_ref[...] = jnp.dot(hid.astype(BF16), wd_ref[0], preferred_element_type=F32)


def _experts(tile_expert, xs, w_gate, w_up, w_down):
    rows, d = xs.shape
    t = EXPERT_TILE
    return pl.pallas_call(
        _expert_kernel,
        grid_spec=pltpu.PrefetchScalarGridSpec(
            num_scalar_prefetch=1,
            grid=(rows // t,),
            in_specs=[pl.BlockSpec((t, d), lambda j, te: (j, 0)),
                      pl.BlockSpec((1, d, EXPERT_FF), lambda j, te: (te[j], 0, 0)),
                      pl.BlockSpec((1, d, EXPERT_FF), lambda j, te: (te[j], 0, 0)),
                      pl.BlockSpec((1, EXPERT_FF, d), lambda j, te: (te[j], 0, 0))],
            out_specs=pl.BlockSpec((t, d), lambda j, te: (j, 0))),
        out_shape=jax.ShapeDtypeStruct((rows, d), F32),
        compiler_params=_params(1),
        name="routed_experts",
    )(tile_expert, xs, w_gate, w_up, w_down)


def _combine_kernel(pos_ref, y_ref, route_ref, x1_ref, g2_ref, pg_ref, o_ref, buf, sem):
    t = COMBINE_TILE

    def issue(r, carry):
        for k in range(2):
            _row_copy(y_ref, pos_ref[0, 0, 2 * r + k], buf.at[k], r, sem).start()
        return carry

    lax.fori_loop(0, t, issue, 0)

    def drain(r, carry):
        for k in range(2):
            _row_copy(y_ref, pos_ref[0, 0, 2 * r + k], buf.at[k], r, sem).wait()
        return carry

    lax.fori_loop(0, t, drain, 0)

    route = route_ref[0]
    w1 = route[:, ROUTE_W1:ROUTE_W1 + 1]
    w2 = route[:, ROUTE_W2:ROUTE_W2 + 1]
    y = w1 * buf[0] + w2 * buf[1]
    o_ref[0] = x1_ref[0] + g2_ref[0] * (_rms(y) * pg_ref[...])


def _combine(pos, ys, route, x1, g2, post_ffn_gain):
    b, s, d = x1.shape
    t = COMBINE_TILE
    nt = s // t
    tok = lambda width: pl.BlockSpec((1, t, width), lambda bi, i: (bi, i, 0))
    return pl.pallas_call(
        _combine_kernel,
        grid=(b, nt),
        in_specs=[pl.BlockSpec((1, 1, 2 * t), lambda bi, i: (bi * nt + i, 0, 0),
                               memory_space=pltpu.SMEM),
                  pl.BlockSpec(memory_space=pl.ANY),
                  tok(LANES), tok(d),
                  pl.BlockSpec((1, 1, d), lambda bi, i: (bi, 0, 0)),
                  pl.BlockSpec((1, d), lambda bi, i: (0, 0))],
        out_specs=tok(d),
        out_shape=jax.ShapeDtypeStruct((b, s, d), F32),
        scratch_shapes=[pltpu.VMEM((2, t, d), F32), pltpu.SemaphoreType.DMA(())],
        compiler_params=_params(2),
        name="moe_combine",
    )(pos.reshape(b * nt, 1, 2 * t), ys, route, x1, g2, post_ffn_gain)


def _routing_tables(route, totals, n_rows):
    t = EXPERT_TILE
    tot = totals[0, :N_EXPERTS].astype(jnp.int32)
    padded = (tot + t - 1) // t * t
    ends = jnp.cumsum(padded)
    base = ends - padded
    rec = route.reshape(-1, LANES)
    e = rec[:, ROUTE_E1:ROUTE_E2 + 1].astype(jnp.int32)
    r = rec[:, ROUTE_R1:ROUTE_R2 + 1].astype(jnp.int32)
    pos = base[e] + r
    tile_start = jnp.arange(n_rows // t, dtype=jnp.int32) * t
    tile_expert = jnp.minimum(jnp.sum(tile_start[:, None] >= ends[None, :], axis=1),
                              N_EXPERTS - 1).astype(jnp.int32)
    return pos.reshape(-1), tile_expert


def _augment_w_in(w_in):
    d = w_in.shape[0]
    scale = HEAD_DIM ** -0.5 * LOG2E
    dq, dk, dv, mq, mk, mv = jnp.split(w_in, [512, 1024, 1536, 2048, 2560], axis=1)

    def pad_heads(w):
        w = w.reshape(d, N_MOBA_HEADS, HEAD_DIM)
        return jnp.pad(w, ((0, 0), (0, 0), (0, LANES - HEAD_DIM))).reshape(d, N_MOBA_HEADS * LANES)

    return jnp.concatenate([dq * scale, dk, dv, pad_heads(mq * scale), pad_heads(mk), mv],
                           axis=1).astype(BF16)


def kernel(x, c, w_ada, b_ada, pre_mix_gain, post_mix_gain, pre_ffn_gain, post_ffn_gain, w_in, lambda_q1, lambda_k1, lambda_q2, lambda_k2, diff_head_gain, w_out, rel_bias, w_group, b_group, w_expert, b_expert, w_gate, w_up, w_down):
    b, s, d = x.shape
    depth = w_in.shape[0]
    tab_t = rel_bias.T
    bias_diff = _bias_tiles(tab_t, ATTN_TILE, 0, N_DIFF_HEADS)
    bias_moba = _bias_tiles(tab_t, ATTN_TILE, N_DIFF_HEADS, N_MOBA_HEADS)
    c_pad = jnp.pad(c, ((0, 8 - b), (0, 0)))
    for l in range(depth):
        lambda_init = 0.8 - 0.6 * math.exp(-0.3 * l)
        mod = _ada(c_pad, w_ada[l], b_ada[l][None])[:b]
        sh1, sc1, g1, sh2, sc2, g2 = [m[:, None, :] for m in jnp.split(mod, 6, axis=-1)]

        dq, dk, dv, mq, mk, mv, kmean = _inproj(x, pre_mix_gain[l][None], sc1, sh1,
                                                _augment_w_in(w_in[l]))
        y_diff = _diff_attention(dq, dk, dv, bias_diff, lambda_q1[l][:, None], lambda_k1[l][:, None],
                                 lambda_q2[l][:, None], lambda_k2[l][:, None],
                                 diff_head_gain[l][:, None], lambda_init)
        nkb = s // MOBA_BLOCK
        km = kmean.reshape(b, nkb, N_MOBA_HEADS, LANES)[..., :HEAD_DIM]
        km = jnp.pad(km.transpose(0, 2, 1, 3),
                     ((0, 0), (0, 0), (HEAD_DIM, LANES - HEAD_DIM - nkb), (0, LANES - HEAD_DIM)))
        y_moba = _moba_attention(mq, mk, mv, km, bias_moba)

        w_router = jnp.pad(jnp.concatenate([w_expert[l], w_group[l]], axis=1),
                           ((0, 0), (0, LANES - N_EXPERTS - N_GROUPS)))
        b_router = jnp.pad(jnp.concatenate([b_expert[l], b_group[l]]),
                           (0, LANES - N_EXPERTS - N_GROUPS))[None]
        x1, h2, route, totals = _mid(
            y_diff, y_moba, x, w_out[l].astype(BF16), post_mix_gain[l][None], g1,
            pre_ffn_gain[l][None], sc2, sh2, w_router, b_router)
        n_rows = 2 * b * s + N_EXPERTS * EXPERT_TILE
        pos, tile_expert = _routing_tables(route, totals, n_rows)
        xs = _dispatch(pos, h2.reshape(b * s, d), jnp.zeros((n_rows, d), F32))
        ys = _experts(tile_expert, xs, w_gate[l].astype(BF16), w_up[l].astype(BF16),
                      w_down[l].astype(BF16))
        x = _combine(pos, ys, route, x1, g2, post_ffn_gain[l][None])
    return x
```

```python
import functools
import math

import jax
import jax.numpy as jnp
from jax import lax
from jax.experimental import pallas as pl
from jax.experimental.pallas import tpu as pltpu

F32 = jnp.float32
BF16 = jnp.bfloat16
HIGHEST = lax.Precision.HIGHEST

D_MODEL = 1024
HEAD_DIM = 64
N_DIFF_HEADS = 4
N_MOBA_HEADS = 8
DIFF_COLS = 512
MOBA_COLS = 512
MOBA_BLOCK = 256
MOBA_TOPK = 3
N_BUCKETS = 32
MAX_EXACT = N_BUCKETS // 2
MAX_DISTANCE = 128
N_GROUPS = 4
EXPERTS_PER_GROUP = 8
N_EXPERTS = 32
EXPERT_FF = 512
EPS = 1e-6
NEG = -1e30
LANES = 128
LOG2E = math.log2(math.e)
ONES_ROWS = 16
DIFF_V_ROWS = 2 * HEAD_DIM + ONES_ROWS
MOBA_V_ROWS = HEAD_DIM + ONES_ROWS

IN_TILE = 512
ATTN_TILE = 512
FAR_UNROLL = 4
MID_TILE = 512
DISPATCH_TILE = 512
EXPERT_TILE = 256
COMBINE_TILE = 256
ROW_DMA_UNROLL = 8
VMEM_LIMIT = 56 * 1024 * 1024


def _params(n_axes, vmem=VMEM_LIMIT):
    return pltpu.CompilerParams(dimension_semantics=("arbitrary",) * n_axes,
                                vmem_limit_bytes=vmem)


def _rms(v):
    return v * lax.rsqrt(jnp.mean(v * v, axis=-1, keepdims=True) + EPS)


def _ada_kernel(c_ref, w_ref, b_ref, o_ref):
    c = c_ref[...]
    s = c * jax.nn.sigmoid(c)
    o_ref[...] = jnp.dot(s, w_ref[...], precision=HIGHEST,
                         preferred_element_type=F32) + b_ref[...]


def _ada(c_pad, w_ada, b_ada):
    n = w_ada.shape[1]
    return pl.pallas_call(
        _ada_kernel,
        grid=(n // D_MODEL,),
        in_specs=[pl.BlockSpec((8, D_MODEL), lambda j: (0, 0)),
                  pl.BlockSpec((D_MODEL, D_MODEL), lambda j: (0, j)),
                  pl.BlockSpec((1, D_MODEL), lambda j: (0, j))],
        out_specs=pl.BlockSpec((8, D_MODEL), lambda j: (0, j)),
        out_shape=jax.ShapeDtypeStruct((8, n), F32),
        compiler_params=_params(1),
        name="ada_mod",
    )(c_pad, w_ada, b_ada)


def _bias_kernel(tab_ref, o_ref, *, tile, head0):
    w = pl.program_id(0)
    h = pl.program_id(1) + head0
    r = lax.broadcasted_iota(jnp.int32, (tile, tile), 0)
    c = lax.broadcasted_iota(jnp.int32, (tile, tile), 1)
    d = c - r + w * tile
    n = jnp.maximum(d, 0)
    nf = jnp.maximum(n, 1).astype(F32)
    large = MAX_EXACT + (jnp.log(nf / MAX_EXACT) / math.log(MAX_DISTANCE / MAX_EXACT)
                         * (N_BUCKETS - MAX_EXACT)).astype(jnp.int32)
    large = jnp.minimum(large, N_BUCKETS - 1)
    bucket = jnp.where(n < MAX_EXACT, n, large)
    far = tab_ref[h, N_BUCKETS - 1]
    val = jnp.zeros((tile, tile), F32)
    for b in range(N_BUCKETS - 1):
        val = jnp.where(bucket == b, (tab_ref[h, b] - far) * LOG2E, val)
    o_ref[0, 0] = jnp.where(d >= 0, val, NEG)


def _bias_tiles(tab_t, tile, head0, n_heads):
    return pl.pallas_call(
        functools.partial(_bias_kernel, tile=tile, head0=head0),
        grid=(2, n_heads),
        in_specs=[pl.BlockSpec(memory_space=pltpu.SMEM)],
        out_specs=pl.BlockSpec((1, 1, tile, tile), lambda w, h: (w, h, 0, 0)),
        out_shape=jax.ShapeDtypeStruct((2, n_heads, tile, tile), F32),
        compiler_params=_params(2),
        name="bias_tiles",
    )(tab_t)


W_DQ, W_DK, W_DV = 0, 512, 1024
W_MQ, W_MK, W_MV = 1536, 2560, 3584
W_END = 4096


def _inproj_kernel(x_ref, gain_ref, sc_ref, sh_ref, w_ref,
                   dq_ref, dk_ref, dv_ref, mq_ref, mk_ref, mv_ref, km_ref):
    i = pl.program_id(1)
    h = _rms(x_ref[0]) * gain_ref[...]
    h = h * (1.0 + sc_ref[0]) + sh_ref[0]
    hb = h.astype(BF16)

    def proj(lo, hi):
        return jnp.dot(hb, w_ref[:, lo:hi], preferred_element_type=F32)

    ones = jnp.ones((ONES_ROWS, IN_TILE), F32)

    def with_ones(vt, rows):
        parts = []
        for r0 in range(0, vt.shape[0], rows):
            parts += [vt[r0:r0 + rows], ones]
        return jnp.concatenate(parts, axis=0).astype(BF16)

    dq_ref[0] = proj(W_DQ, W_DK).T.astype(BF16)
    dk_ref[0] = proj(W_DK, W_DV).astype(BF16)
    dv_ref[0] = with_ones(proj(W_DV, W_MQ).T, 2 * HEAD_DIM)
    mq_ref[0] = proj(W_MQ, W_MK).T.astype(BF16)
    mv_ref[0] = with_ones(proj(W_MV, W_END).T, HEAD_DIM)
    mk = proj(W_MK, W_MV)
    for r in range(IN_TILE // MOBA_BLOCK):
        km_ref[0, r] = jnp.mean(mk[r * MOBA_BLOCK:(r + 1) * MOBA_BLOCK], axis=0, keepdims=True)
    row = lax.broadcasted_iota(jnp.int32, mk.shape, 0)
    lane = lax.broadcasted_iota(jnp.int32, mk.shape, 1)
    blk = (i * IN_TILE + row) // MOBA_BLOCK
    onehot = (lane % LANES) - HEAD_DIM == blk
    mk_ref[0] = jnp.where(onehot, 1.0, mk).astype(BF16)


def _inproj(x, gain, sc1, sh1, w_aug):
    b, s, d = x.shape
    nkb = s // MOBA_BLOCK
    tok = lambda width: pl.BlockSpec((1, IN_TILE, width), lambda bi, i: (bi, i, 0))
    tok_t = lambda rows: pl.BlockSpec((1, rows, IN_TILE), lambda bi, i: (bi, 0, i))
    vec = pl.BlockSpec((1, 1, d), lambda bi, i: (bi, 0, 0))
    shp = lambda width: jax.ShapeDtypeStruct((b, s, width), BF16)
    shp_t = lambda rows: jax.ShapeDtypeStruct((b, rows, s), BF16)
    dv_rows = N_DIFF_HEADS * DIFF_V_ROWS
    mv_rows = N_MOBA_HEADS * MOBA_V_ROWS
    return pl.pallas_call(
        _inproj_kernel,
        grid=(b, s // IN_TILE),
        in_specs=[tok(d), pl.BlockSpec((1, d), lambda bi, i: (0, 0)), vec, vec,
                  pl.BlockSpec((d, W_END), lambda bi, i: (0, 0))],
        out_specs=[tok_t(512), tok(512), tok_t(dv_rows), tok_t(1024), tok(1024), tok_t(mv_rows),
                   pl.BlockSpec((1, IN_TILE // MOBA_BLOCK, 1, 1024), lambda bi, i: (bi, i, 0, 0))],
        out_shape=[shp_t(512), shp(512), shp_t(dv_rows), shp_t(1024), shp(1024), shp_t(mv_rows),
                   jax.ShapeDtypeStruct((b, nkb, 1, 1024), F32)],
        compiler_params=_params(2),
        name="in_proj",
    )(x, gain, sc1, sh1, w_aug)


def _softmax_step(st, vt, m_ref, acc_ref, idx):
    m_prev = m_ref[idx]
    m_next = jnp.maximum(m_prev, jnp.max(st, axis=0, keepdims=True))
    alpha = jnp.exp2(m_prev - m_next)
    p = jnp.exp2(st - m_next).astype(BF16)
    acc_ref[idx] = alpha * acc_ref[idx] + jnp.dot(vt, p, preferred_element_type=F32)
    m_ref[idx] = m_next


def _softmax_result(acc_ref, idx, rows):
    acc = acc_ref[idx]
    return acc[:rows] / acc[rows:rows + 1]


def _init_softmax_state(m_s, acc_s):
    m_s[...] = jnp.full(m_s.shape, -jnp.inf, F32)
    acc_s[...] = jnp.zeros(acc_s.shape, F32)


def _causal_sweep(i, scores, consume, s_scr):
    def direct(j, which):
        for m in range(2):
            consume(j, m, scores(j, m, which))

    direct(i, 0)

    @pl.when(i > 0)
    def _():
        direct(i - 1, 1)

    n_far = jnp.maximum(i - 1, 0)

    def produce(j, slot):
        for m in range(2):
            s_scr[slot, m] = scores(j, m, None)

    def staged(j, slot):
        for m in range(2):
            consume(j, m, s_scr[slot, m])

    @pl.when(n_far > 0)
    def _():
        produce(0, 0)

    def far_group(first, count):
        for u in range(count):
            produce(jnp.minimum(first + u + 1, n_far - 1), (u + 1) % 2)
            staged(first + u, u % 2)

    def far_loop(jj, carry):
        far_group(FAR_UNROLL * jj, FAR_UNROLL)
        return carry

    lax.fori_loop(0, n_far // FAR_UNROLL, far_loop, 0)

    left = n_far % FAR_UNROLL
    first_left = n_far - left

    @pl.when(left >= 2)
    def _():
        far_group(first_left, 2)

    @pl.when(left % 2 == 1)
    def _():
        staged(n_far - 1, 0)


def _attn_scratch(v_rows):
    t = ATTN_TILE
    return [pltpu.VMEM((2, LANES, t), BF16), pltpu.VMEM((2, 1, t), F32),
            pltpu.VMEM((2, v_rows, t), F32), pltpu.VMEM((2, 2, t, t), F32)]


def _diff_kernel(lq1_ref, lk1_ref, lq2_ref, lk2_ref, q_ref, k_ref, v_ref, bias_ref, gain_ref,
                 o_ref, qs, m_s, acc_s, s_scr, *, lambda_init):
    i = pl.program_id(2)
    t = ATTN_TILE
    q = q_ref[0]
    row = lax.broadcasted_iota(jnp.int32, q.shape, 0)
    zero = jnp.zeros_like(q)
    qs[0] = jnp.where(row < HEAD_DIM, q, zero)
    qs[1] = jnp.where(row >= HEAD_DIM, q, zero)
    _init_softmax_state(m_s, acc_s)

    def scores(j, m, which):
        k = k_ref[0, pl.ds(pl.multiple_of(j * t, t), t), :]
        s = jnp.dot(k, qs[m], preferred_element_type=F32)
        return s if which is None else s + bias_ref[which, 0]

    def consume(j, m, st):
        vt = v_ref[0, :, pl.ds(pl.multiple_of(j * t, t), t)]
        _softmax_step(st, vt, m_s, acc_s, m)

    _causal_sweep(i, scores, consume, s_scr)

    lam = (jnp.exp(jnp.sum(lq1_ref[...] * lk1_ref[...], axis=0, keepdims=True))
           - jnp.exp(jnp.sum(lq2_ref[...] * lk2_ref[...], axis=0, keepdims=True)) + lambda_init)
    dv = 2 * HEAD_DIM
    a = _softmax_result(acc_s, 0, dv) - lam * _softmax_result(acc_s, 1, dv)
    y = a * lax.rsqrt(jnp.mean(a * a, axis=0, keepdims=True) + EPS)
    o_ref[0] = (y * gain_ref[...] * (1.0 - lambda_init)).astype(BF16)


def _diff_attention(dq_t, dk, dv_t, bias, lq1, lk1, lq2, lk2, head_gain, lambda_init):
    b, _, s = dq_t.shape
    t = ATTN_TILE
    col = lambda rows: pl.BlockSpec((rows, 1), lambda bi, h, i: (0, 0))
    return pl.pallas_call(
        functools.partial(_diff_kernel, lambda_init=lambda_init),
        grid=(b, N_DIFF_HEADS, s // t),
        in_specs=[col(HEAD_DIM), col(HEAD_DIM), col(HEAD_DIM), col(HEAD_DIM),
                  pl.BlockSpec((1, LANES, t), lambda bi, h, i: (bi, h, i)),
                  pl.BlockSpec((1, s, LANES), lambda bi, h, i: (bi, 0, h)),
                  pl.BlockSpec((1, DIFF_V_ROWS, s), lambda bi, h, i: (bi, h, 0)),
                  pl.BlockSpec((2, 1, t, t), lambda bi, h, i: (0, h, 0, 0)),
                  col(2 * HEAD_DIM)],
        out_specs=pl.BlockSpec((1, LANES, t), lambda bi, h, i: (bi, h, i)),
        out_shape=jax.ShapeDtypeStruct((b, DIFF_COLS, s), BF16),
        scratch_shapes=_attn_scratch(DIFF_V_ROWS),
        compiler_params=_params(3),
        name="diff_attn",
    )(lq1, lk1, lq2, lk2, dq_t, dk, dv_t, bias, head_gain)


def _moba_kernel(q_ref, k_ref, v_ref, km_ref, bias_ref, o_ref, qs, m_s, acc_s, s_scr):
    i = pl.program_id(2)
    t = ATTN_TILE
    row = lax.broadcasted_iota(jnp.int32, (LANES, t), 0)
    lane = lax.broadcasted_iota(jnp.int32, (LANES, t), 1)
    blk = row - HEAD_DIM
    own = i * (t // MOBA_BLOCK) + lane // MOBA_BLOCK
    for hh in range(2):
        qa = q_ref[0, hh * LANES:(hh + 1) * LANES, :]
        g = jnp.dot(km_ref[0, hh], qa.astype(F32), precision=HIGHEST,
                    preferred_element_type=F32)
        g = jnp.where((blk >= 0) & (blk < own), g, -jnp.inf)
        sel = blk == own
        for _ in range(MOBA_TOPK):
            mx = jnp.max(g, axis=0, keepdims=True)
            first = jnp.min(jnp.where(g == mx, row, 2 * LANES), axis=0, keepdims=True)
            pick = (row == first) & (mx > -jnp.inf)
            sel = sel | pick
            g = jnp.where(pick, -jnp.inf, g)
        gate = jnp.where(sel, 0.0, NEG).astype(BF16)
        qs[hh] = jnp.where(row < HEAD_DIM, qa, gate)
    _init_softmax_state(m_s, acc_s)

    def scores(j, hh, which):
        k = k_ref[0, pl.ds(pl.multiple_of(j * t, t), t), hh * LANES:(hh + 1) * LANES]
        s = jnp.dot(k, qs[hh], preferred_element_type=F32)
        return s if which is None else s + bias_ref[which, hh]

    def consume(j, hh, st):
        vt = v_ref[0, hh * MOBA_V_ROWS:(hh + 1) * MOBA_V_ROWS, pl.ds(pl.multiple_of(j * t, t), t)]
        _softmax_step(st, vt, m_s, acc_s, hh)

    _causal_sweep(i, scores, consume, s_scr)

    o = jnp.concatenate([_softmax_result(acc_s, 0, HEAD_DIM), _softmax_result(acc_s, 1, HEAD_DIM)],
                        axis=0)
    o_ref[0] = o.astype(BF16)


def _moba_attention(mq_t, mk, mv_t, km, bias):
    b, _, s = mq_t.shape
    t = ATTN_TILE
    return pl.pallas_call(
        _moba_kernel,
        grid=(b, N_MOBA_HEADS // 2, s // t),
        in_specs=[pl.BlockSpec((1, 2 * LANES, t), lambda bi, p, i: (bi, p, i)),
                  pl.BlockSpec((1, s, 2 * LANES), lambda bi, p, i: (bi, 0, p)),
                  pl.BlockSpec((1, 2 * MOBA_V_ROWS, s), lambda bi, p, i: (bi, p, 0)),
                  pl.BlockSpec((1, 2, LANES, LANES), lambda bi, p, i: (bi, p, 0, 0)),
                  pl.BlockSpec((2, 2, t, t), lambda bi, p, i: (0, p, 0, 0))],
        out_specs=pl.BlockSpec((1, LANES, t), lambda bi, p, i: (bi, p, i)),
        out_shape=jax.ShapeDtypeStruct((b, MOBA_COLS, s), BF16),
        scratch_shapes=_attn_scratch(MOBA_V_ROWS),
        compiler_params=_params(3),
        name="moba_attn",
    )(mq_t, mk, mv_t, km, bias)


ROUTER_GROUP_LANE = N_EXPERTS
ROUTE_E1, ROUTE_E2, ROUTE_R1, ROUTE_R2, ROUTE_W1, ROUTE_W2 = range(6)


def _mid_kernel(yd_ref, ym_ref, x_ref, wo_ref, pg_ref, g1_ref, fg_ref, sc2_ref, sh2_ref,
                wr_ref, br_ref, x1_ref, h2_ref, route_ref, tot_ref, run):
    first = (pl.program_id(0) == 0) & (pl.program_id(1) == 0)

    @pl.when(first)
    def _():
        run[...] = jnp.zeros(run.shape, F32)

    tn = (((0,), (0,)), ((), ()))
    y = (lax.dot_general(yd_ref[0], wo_ref[0:DIFF_COLS], tn, preferred_element_type=F32)
         + lax.dot_general(ym_ref[0], wo_ref[DIFF_COLS:D_MODEL], tn, preferred_element_type=F32))
    x1 = x_ref[0] + g1_ref[0] * (_rms(y) * pg_ref[...])
    x1_ref[0] = x1
    h2 = _rms(x1) * fg_ref[...] * (1.0 + sc2_ref[0]) + sh2_ref[0]
    h2_ref[0] = h2

    logits = jnp.dot(h2, wr_ref[...], precision=HIGHEST, preferred_element_type=F32) + br_ref[...]
    lane = lax.broadcasted_iota(jnp.int32, logits.shape, 1)
    is_group = (lane >= ROUTER_GROUP_LANE) & (lane < ROUTER_GROUP_LANE + N_GROUPS)
    gl = jnp.where(is_group, logits, -jnp.inf)
    gmax = jnp.max(gl, axis=1, keepdims=True)
    g_idx = jnp.min(jnp.where(gl == gmax, lane - ROUTER_GROUP_LANE, N_GROUPS), axis=1, keepdims=True)
    g_w = 1.0 / jnp.sum(jnp.exp(gl - gmax), axis=1, keepdims=True)
    in_group = (lane < N_EXPERTS) & (lane // EXPERTS_PER_GROUP == g_idx)
    el = jnp.where(in_group, logits, -jnp.inf)
    m1 = jnp.max(el, axis=1, keepdims=True)
    i1 = jnp.min(jnp.where(el == m1, lane, LANES), axis=1, keepdims=True)
    el2 = jnp.where(lane == i1, -jnp.inf, el)
    m2 = jnp.max(el2, axis=1, keepdims=True)
    i2 = jnp.min(jnp.where(el2 == m2, lane, LANES), axis=1, keepdims=True)
    e2 = jnp.exp(m2 - m1)
    w1 = g_w / (1.0 + e2)
    w2 = g_w * e2 / (1.0 + e2)

    t = MID_TILE
    used = jnp.where((lane == i1) | (lane == i2), 1.0, 0.0)
    earlier = jnp.where(lax.broadcasted_iota(jnp.int32, (t, t), 1)
                        < lax.broadcasted_iota(jnp.int32, (t, t), 0), 1.0, 0.0).astype(BF16)
    rank = jnp.dot(earlier, used.astype(BF16), preferred_element_type=F32) + run[...]
    r1 = jnp.sum(jnp.where(lane == i1, rank, 0.0), axis=1, keepdims=True)
    r2 = jnp.sum(jnp.where(lane == i2, rank, 0.0), axis=1, keepdims=True)
    record = jnp.zeros(logits.shape, F32)
    for field, val in ((ROUTE_E1, i1.astype(F32)), (ROUTE_E2, i2.astype(F32)), (ROUTE_R1, r1),
                       (ROUTE_R2, r2), (ROUTE_W1, w1), (ROUTE_W2, w2)):
        record = jnp.where(lane == field, val, record)
    route_ref[0] = record
    run[...] += jnp.sum(used, axis=0, keepdims=True)
    tot_ref[...] = run[...]


def _mid(yd, ym, x, w_out, post_mix_gain, g1, pre_ffn_gain, sc2, sh2, w_router, b_router):
    b, s, d = x.shape
    t = MID_TILE
    tok = lambda width: pl.BlockSpec((1, t, width), lambda bi, i: (bi, i, 0))
    vec = pl.BlockSpec((1, 1, d), lambda bi, i: (bi, 0, 0))
    row = lambda width: pl.BlockSpec((1, width), lambda bi, i: (0, 0))
    return pl.pallas_call(
        _mid_kernel,
        grid=(b, s // t),
        in_specs=[pl.BlockSpec((1, DIFF_COLS, t), lambda bi, i: (bi, 0, i)),
                  pl.BlockSpec((1, MOBA_COLS, t), lambda bi, i: (bi, 0, i)), tok(d),
                  pl.BlockSpec((d, d), lambda bi, i: (0, 0)),
                  row(d), vec, row(d), vec, vec,
                  pl.BlockSpec((d, LANES), lambda bi, i: (0, 0)), row(LANES)],
        out_specs=[tok(d), tok(d), tok(LANES), row(LANES)],
        out_shape=[jax.ShapeDtypeStruct((b, s, d), F32),
                   jax.ShapeDtypeStruct((b, s, d), F32),
                   jax.ShapeDtypeStruct((b, s, LANES), F32),
                   jax.ShapeDtypeStruct((1, LANES), F32)],
        scratch_shapes=[pltpu.VMEM((1, LANES), F32)],
        compiler_params=_params(2),
        name="out_proj_router",
    )(yd, ym, x, w_out, post_mix_gain, g1, pre_ffn_gain, sc2, sh2, w_router, b_router)


def _row_copy(src, src_row, dst, dst_row, sem):
    return pltpu.make_async_copy(src.at[pl.ds(src_row, 1), :], dst.at[pl.ds(dst_row, 1), :], sem)


def _dispatch_kernel(pos_ref, h_ref, xs_in_ref, xs_ref, sem):
    del xs_in_ref

    def issue(t, carry):
        for k in range(2):
            _row_copy(h_ref, t, xs_ref, pos_ref[0, 0, 2 * t + k], sem).start()
        return carry

    lax.fori_loop(0, DISPATCH_TILE, issue, 0, unroll=ROW_DMA_UNROLL)

    def drain(t, carry):
        for k in range(2):
            _row_copy(h_ref, t, xs_ref, pos_ref[0, 0, 2 * t + k], sem).wait()
        return carry

    lax.fori_loop(0, DISPATCH_TILE, drain, 0, unroll=ROW_DMA_UNROLL)


def _dispatch(pos, h2, xs_zero):
    n, d = h2.shape
    t = DISPATCH_TILE
    return pl.pallas_call(
        _dispatch_kernel,
        grid=(n // t,),
        in_specs=[pl.BlockSpec((1, 1, 2 * t), lambda i: (i, 0, 0), memory_space=pltpu.SMEM),
                  pl.BlockSpec((t, d), lambda i: (i, 0)),
                  pl.BlockSpec(memory_space=pl.ANY)],
        out_specs=pl.BlockSpec(memory_space=pl.ANY),
        out_shape=jax.ShapeDtypeStruct(xs_zero.shape, F32),
        scratch_shapes=[pltpu.SemaphoreType.DMA(())],
        input_output_aliases={2: 0},
        compiler_params=_params(1),
        name="moe_dispatch",
    )(pos.reshape(n // t, 1, 2 * t), h2, xs_zero)


def _expert_kernel(te_ref, x_ref, wg_ref, wu_ref, wd_ref, y_ref, wg_s, wu_s, wd_s):
    j = pl.program_id(0)
    new_expert = (j == 0) | (te_ref[j] != te_ref[jnp.maximum(j - 1, 0)])

    @pl.when(new_expert)
    def _():
        wg_s[...] = wg_ref[0].astype(BF16)
        wu_s[...] = wu_ref[0].astype(BF16)
        wd_s[...] = wd_ref[0].astype(BF16)

    x = x_ref[...].astype(BF16)
    a = jnp.dot(x, wg_s[...], preferred_element_type=F32)
    u = jnp.dot(x, wu_s[...], preferred_element_type=F32)
    hid = (a * jax.nn.sigmoid(a)) * u
    y_ref[...] = jnp.dot(hid.astype(BF16), wd_s[...], preferred_element_type=F32)


def _experts(tile_expert, xs, w_gate, w_up, w_down):
    rows, d = xs.shape
    t = EXPERT_TILE
    return pl.pallas_call(
        _expert_kernel,
        grid_spec=pltpu.PrefetchScalarGridSpec(
            num_scalar_prefetch=1,
            grid=(rows // t,),
            in_specs=[pl.BlockSpec((t, d), lambda j, te: (j, 0)),
                      pl.BlockSpec((1, d, EXPERT_FF), lambda j, te: (te[j], 0, 0)),
                      pl.BlockSpec((1, d, EXPERT_FF), lambda j, te: (te[j], 0, 0)),
                      pl.BlockSpec((1, EXPERT_FF, d), lambda j, te: (te[j], 0, 0))],
            out_specs=pl.BlockSpec((t, d), lambda j, te: (j, 0)),
            scratch_shapes=[pltpu.VMEM((d, EXPERT_FF), BF16), pltpu.VMEM((d, EXPERT_FF), BF16),
                            pltpu.VMEM((EXPERT_FF, d), BF16)]),
        out_shape=jax.ShapeDtypeStruct((rows, d), F32),
        compiler_params=_params(1),
        name="routed_experts",
    )(tile_expert, xs, w_gate, w_up, w_down)


def _combine_kernel(pos_ref, y_ref, route_ref, x1_ref, g2_ref, pg_ref, o_ref, buf, sem):
    t = COMBINE_TILE

    def issue(r, carry):
        for k in range(2):
            _row_copy(y_ref, pos_ref[0, 0, 2 * r + k], buf.at[k], r, sem).start()
        return carry

    lax.fori_loop(0, t, issue, 0, unroll=ROW_DMA_UNROLL)

    def drain(r, carry):
        for k in range(2):
            _row_copy(y_ref, pos_ref[0, 0, 2 * r + k], buf.at[k], r, sem).wait()
        return carry

    lax.fori_loop(0, t, drain, 0, unroll=ROW_DMA_UNROLL)

    route = route_ref[0]
    w1 = route[:, ROUTE_W1:ROUTE_W1 + 1]
    w2 = route[:, ROUTE_W2:ROUTE_W2 + 1]
    y = w1 * buf[0] + w2 * buf[1]
    o_ref[0] = x1_ref[0] + g2_ref[0] * (_rms(y) * pg_ref[...])


def _combine(pos, ys, route, x1, g2, post_ffn_gain):
    b, s, d = x1.shape
    t = COMBINE_TILE
    nt = s // t
    tok = lambda width: pl.BlockSpec((1, t, width), lambda bi, i: (bi, i, 0))
    return pl.pallas_call(
        _combine_kernel,
        grid=(b, nt),
        in_specs=[pl.BlockSpec((1, 1, 2 * t), lambda bi, i: (bi * nt + i, 0, 0),
                               memory_space=pltpu.SMEM),
                  pl.BlockSpec(memory_space=pl.ANY),
                  tok(LANES), tok(d),
                  pl.BlockSpec((1, 1, d), lambda bi, i: (bi, 0, 0)),
                  pl.BlockSpec((1, d), lambda bi, i: (0, 0))],
        out_specs=tok(d),
        out_shape=jax.ShapeDtypeStruct((b, s, d), F32),
        scratch_shapes=[pltpu.VMEM((2, t, d), F32), pltpu.SemaphoreType.DMA(())],
        compiler_params=_params(2),
        name="moe_combine",
    )(pos.reshape(b * nt, 1, 2 * t), ys, route, x1, g2, post_ffn_gain)


def _routing_tables(route, totals, n_rows):
    t = EXPERT_TILE
    tot = totals[0, :N_EXPERTS].astype(jnp.int32)
    padded = (tot + t - 1) // t * t
    ends = jnp.cumsum(padded)
    base = ends - padded
    rec = route.reshape(-1, LANES)
    e = rec[:, ROUTE_E1:ROUTE_E2 + 1].astype(jnp.int32)
    r = rec[:, ROUTE_R1:ROUTE_R2 + 1].astype(jnp.int32)
    pos = base[e] + r
    tile_start = jnp.arange(n_rows // t, dtype=jnp.int32) * t
    tile_expert = jnp.minimum(jnp.sum(tile_start[:, None] >= ends[None, :], axis=1),
                              N_EXPERTS - 1).astype(jnp.int32)
    return pos.reshape(-1), tile_expert


def _augment_w_in(w_in):
    d = w_in.shape[0]
    scale = HEAD_DIM ** -0.5 * LOG2E
    dq, dk, dv, mq, mk, mv = jnp.split(w_in, [512, 1024, 1536, 2048, 2560], axis=1)

    def pad_heads(w):
        w = w.reshape(d, N_MOBA_HEADS, HEAD_DIM)
        return jnp.pad(w, ((0, 0), (0, 0), (0, LANES - HEAD_DIM))).reshape(d, N_MOBA_HEADS * LANES)

    return jnp.concatenate([dq * scale, dk, dv, pad_heads(mq * scale), pad_heads(mk), mv],
                           axis=1).astype(BF16)


def kernel(x, c, w_ada, b_ada, pre_mix_gain, post_mix_gain, pre_ffn_gain, post_ffn_gain, w_in, lambda_q1, lambda_k1, lambda_q2, lambda_k2, diff_head_gain, w_out, rel_bias, w_group, b_group, w_expert, b_expert, w_gate, w_up, w_down):
    b, s, d = x.shape
    depth = w_in.shape[0]
    tab_t = rel_bias.T
    bias_diff = _bias_tiles(tab_t, ATTN_TILE, 0, N_DIFF_HEADS)
    bias_moba = _bias_tiles(tab_t, ATTN_TILE, N_DIFF_HEADS, N_MOBA_HEADS)
    c_pad = jnp.pad(c, ((0, 8 - b), (0, 0)))
    for l in range(depth):
        lambda_init = 0.8 - 0.6 * math.exp(-0.3 * l)
        mod = _ada(c_pad, w_ada[l], b_ada[l][None])[:b]
        sh1, sc1, g1, sh2, sc2, g2 = [m[:, None, :] for m in jnp.split(mod, 6, axis=-1)]

        dq, dk, dv, mq, mk, mv, kmean = _inproj(x, pre_mix_gain[l][None], sc1, sh1,
                                                _augment_w_in(w_in[l]))
        y_diff = _diff_attention(dq, dk, dv, bias_diff, lambda_q1[l][:, None], lambda_k1[l][:, None],
                                 lambda_q2[l][:, None], lambda_k2[l][:, None],
                                 diff_head_gain[l][:, None], lambda_init)
        nkb = s // MOBA_BLOCK
        km = kmean.reshape(b, nkb, N_MOBA_HEADS, LANES)[..., :HEAD_DIM]
        km = jnp.pad(km.transpose(0, 2, 1, 3),
                     ((0, 0), (0, 0), (HEAD_DIM, LANES - HEAD_DIM - nkb), (0, LANES - HEAD_DIM)))
        y_moba = _moba_attention(mq, mk, mv, km, bias_moba)

        w_router = jnp.pad(jnp.concatenate([w_expert[l], w_group[l]], axis=1),
                           ((0, 0), (0, LANES - N_EXPERTS - N_GROUPS)))
        b_router = jnp.pad(jnp.concatenate([b_expert[l], b_group[l]]),
                           (0, LANES - N_EXPERTS - N_GROUPS))[None]
        x1, h2, route, totals = _mid(
            y_diff, y_moba, x, w_out[l].astype(BF16), post_mix_gain[l][None], g1,
            pre_ffn_gain[l][None], sc2, sh2, w_router, b_router)
        n_rows = 2 * b * s + N_EXPERTS * EXPERT_TILE
        pos, tile_expert = _routing_tables(route, totals, n_rows)
        xs = _dispatch(pos, h2.reshape(b * s, d), jnp.zeros((n_rows, d), F32))
        ys = _experts(tile_expert, xs, w_gate[l], w_up[l], w_down[l])
        x = _combine(pos, ys, route, x1, g2, post_ffn_gain[l][None])
    return x
```

```python
import functools
import math

import jax
import jax.numpy as jnp
from jax import lax
from jax.experimental import pallas as pl
from jax.experimental.pallas import tpu as pltpu

F32 = jnp.float32
BF16 = jnp.bfloat16
HIGHEST = lax.Precision.HIGHEST

D_MODEL = 1024
HEAD_DIM = 64
N_DIFF_HEADS = 4
N_MOBA_HEADS = 8
DIFF_COLS = 512
MOBA_COLS = 512
MOBA_BLOCK = 256
MOBA_TOPK = 3
N_BUCKETS = 32
MAX_EXACT = N_BUCKETS // 2
MAX_DISTANCE = 128
N_GROUPS = 4
EXPERTS_PER_GROUP = 8
N_EXPERTS = 32
EXPERT_FF = 512
EPS = 1e-6
NEG = -1e30
LANES = 128
LOG2E = math.log2(math.e)
ONES_ROWS = 16
DIFF_V_ROWS = 2 * HEAD_DIM + ONES_ROWS
MOBA_V_ROWS = HEAD_DIM + ONES_ROWS

IN_TILE = 512
ATTN_TILE = 512
FAR_UNROLL = 8
MID_TILE = 512
DISPATCH_TILE = 512
EXPERT_TILE = 256
COMBINE_TILE = 256
ROW_DMA_UNROLL = 8
VMEM_LIMIT = 56 * 1024 * 1024


def _params(n_axes, vmem=VMEM_LIMIT):
    return pltpu.CompilerParams(dimension_semantics=("arbitrary",) * n_axes,
                                vmem_limit_bytes=vmem)


def _rms(v):
    return v * lax.rsqrt(jnp.mean(v * v, axis=-1, keepdims=True) + EPS)


def _ada_kernel(c_ref, w_ref, b_ref, o_ref):
    c = c_ref[...]
    s = c * jax.nn.sigmoid(c)
    o_ref[...] = jnp.dot(s, w_ref[...], precision=HIGHEST,
                         preferred_element_type=F32) + b_ref[...]


def _ada(c_pad, w_ada, b_ada):
    n = w_ada.shape[1]
    return pl.pallas_call(
        _ada_kernel,
        grid=(n // D_MODEL,),
        in_specs=[pl.BlockSpec((8, D_MODEL), lambda j: (0, 0)),
                  pl.BlockSpec((D_MODEL, D_MODEL), lambda j: (0, j)),
                  pl.BlockSpec((1, D_MODEL), lambda j: (0, j))],
        out_specs=pl.BlockSpec((8, D_MODEL), lambda j: (0, j)),
        out_shape=jax.ShapeDtypeStruct((8, n), F32),
        compiler_params=_params(1),
        name="ada_mod",
    )(c_pad, w_ada, b_ada)


def _bias_kernel(tab_ref, o_ref, *, tile, head0):
    w = pl.program_id(0)
    h = pl.program_id(1) + head0
    r = lax.broadcasted_iota(jnp.int32, (tile, tile), 0)
    c = lax.broadcasted_iota(jnp.int32, (tile, tile), 1)
    d = c - r + w * tile
    n = jnp.maximum(d, 0)
    nf = jnp.maximum(n, 1).astype(F32)
    large = MAX_EXACT + (jnp.log(nf / MAX_EXACT) / math.log(MAX_DISTANCE / MAX_EXACT)
                         * (N_BUCKETS - MAX_EXACT)).astype(jnp.int32)
    large = jnp.minimum(large, N_BUCKETS - 1)
    bucket = jnp.where(n < MAX_EXACT, n, large)
    far = tab_ref[h, N_BUCKETS - 1]
    val = jnp.zeros((tile, tile), F32)
    for b in range(N_BUCKETS - 1):
        val = jnp.where(bucket == b, (tab_ref[h, b] - far) * LOG2E, val)
    o_ref[0, 0] = jnp.where(d >= 0, val, NEG)


def _bias_tiles(tab_t, tile, head0, n_heads):
    return pl.pallas_call(
        functools.partial(_bias_kernel, tile=tile, head0=head0),
        grid=(2, n_heads),
        in_specs=[pl.BlockSpec(memory_space=pltpu.SMEM)],
        out_specs=pl.BlockSpec((1, 1, tile, tile), lambda w, h: (w, h, 0, 0)),
        out_shape=jax.ShapeDtypeStruct((2, n_heads, tile, tile), F32),
        compiler_params=_params(2),
        name="bias_tiles",
    )(tab_t)


W_DQ, W_DK, W_DV = 0, 512, 1024
W_MQ, W_MK, W_MV = 1536, 2560, 3584
W_END = 4096


def _inproj_kernel(x_ref, gain_ref, sc_ref, sh_ref, w_ref,
                   dq_ref, dk_ref, dv_ref, mq_ref, mk_ref, mv_ref, km_ref):
    i = pl.program_id(1)
    h = _rms(x_ref[0]) * gain_ref[...]
    h = h * (1.0 + sc_ref[0]) + sh_ref[0]
    hb = h.astype(BF16)

    def proj(lo, hi):
        return jnp.dot(hb, w_ref[:, lo:hi], preferred_element_type=F32)

    ones = jnp.ones((ONES_ROWS, IN_TILE), F32)

    def with_ones(vt, rows):
        parts = []
        for r0 in range(0, vt.shape[0], rows):
            parts += [vt[r0:r0 + rows], ones]
        return jnp.concatenate(parts, axis=0).astype(BF16)

    dq_ref[0] = proj(W_DQ, W_DK).T.astype(BF16)
    dk_ref[0] = proj(W_DK, W_DV).astype(BF16)
    dv_ref[0] = with_ones(proj(W_DV, W_MQ).T, 2 * HEAD_DIM)
    mq_ref[0] = proj(W_MQ, W_MK).T.astype(BF16)
    mv_ref[0] = with_ones(proj(W_MV, W_END).T, HEAD_DIM)
    mk = proj(W_MK, W_MV)
    for r in range(IN_TILE // MOBA_BLOCK):
        km_ref[0, r] = jnp.mean(mk[r * MOBA_BLOCK:(r + 1) * MOBA_BLOCK], axis=0, keepdims=True)
    row = lax.broadcasted_iota(jnp.int32, mk.shape, 0)
    lane = lax.broadcasted_iota(jnp.int32, mk.shape, 1)
    blk = (i * IN_TILE + row) // MOBA_BLOCK
    onehot = (lane % LANES) - HEAD_DIM == blk
    mk_ref[0] = jnp.where(onehot, 1.0, mk).astype(BF16)


def _inproj(x, gain, sc1, sh1, w_aug):
    b, s, d = x.shape
    nkb = s // MOBA_BLOCK
    tok = lambda width: pl.BlockSpec((1, IN_TILE, width), lambda bi, i: (bi, i, 0))
    tok_t = lambda rows: pl.BlockSpec((1, rows, IN_TILE), lambda bi, i: (bi, 0, i))
    vec = pl.BlockSpec((1, 1, d), lambda bi, i: (bi, 0, 0))
    shp = lambda width: jax.ShapeDtypeStruct((b, s, width), BF16)
    shp_t = lambda rows: jax.ShapeDtypeStruct((b, rows, s), BF16)
    dv_rows = N_DIFF_HEADS * DIFF_V_ROWS
    mv_rows = N_MOBA_HEADS * MOBA_V_ROWS
    return pl.pallas_call(
        _inproj_kernel,
        grid=(b, s // IN_TILE),
        in_specs=[tok(d), pl.BlockSpec((1, d), lambda bi, i: (0, 0)), vec, vec,
                  pl.BlockSpec((d, W_END), lambda bi, i: (0, 0))],
        out_specs=[tok_t(512), tok(512), tok_t(dv_rows), tok_t(1024), tok(1024), tok_t(mv_rows),
                   pl.BlockSpec((1, IN_TILE // MOBA_BLOCK, 1, 1024), lambda bi, i: (bi, i, 0, 0))],
        out_shape=[shp_t(512), shp(512), shp_t(dv_rows), shp_t(1024), shp(1024), shp_t(mv_rows),
                   jax.ShapeDtypeStruct((b, nkb, 1, 1024), F32)],
        compiler_params=_params(2),
        name="in_proj",
    )(x, gain, sc1, sh1, w_aug)


def _softmax_step(st, vt, m_ref, acc_ref, idx):
    m_prev = m_ref[idx]
    m_next = jnp.maximum(m_prev, jnp.max(st, axis=0, keepdims=True))
    alpha = jnp.exp2(m_prev - m_next)
    p = jnp.exp2(st - m_next).astype(BF16)
    acc_ref[idx] = alpha * acc_ref[idx] + jnp.dot(vt, p, preferred_element_type=F32)
    m_ref[idx] = m_next


def _softmax_result(acc_ref, idx, rows):
    acc = acc_ref[idx]
    return acc[:rows] / acc[rows:rows + 1]


def _init_softmax_state(m_s, acc_s):
    m_s[...] = jnp.full(m_s.shape, -jnp.inf, F32)
    acc_s[...] = jnp.zeros(acc_s.shape, F32)


def _causal_sweep(i, scores, consume, s_scr):
    def direct(j, which):
        for m in range(2):
            consume(j, m, scores(j, m, which))

    @pl.when(i == 0)
    def _():
        direct(i, 0)

    @pl.when(i > 0)
    def _():
        direct(i, 0)
        direct(i - 1, 1)

    n_far = jnp.maximum(i - 1, 0)

    def produce(j, slot):
        for m in range(2):
            s_scr[slot, m] = scores(j, m, None)

    def staged(j, slot):
        for m in range(2):
            consume(j, m, s_scr[slot, m])

    @pl.when(n_far > 0)
    def _():
        produce(0, 0)

    def far_group(first, count):
        for u in range(count):
            produce(jnp.minimum(first + u + 1, n_far - 1), (u + 1) % 2)
            staged(first + u, u % 2)

    def far_loop(jj, carry):
        far_group(FAR_UNROLL * jj, FAR_UNROLL)
        return carry

    lax.fori_loop(0, n_far // FAR_UNROLL, far_loop, 0)

    left = n_far % FAR_UNROLL
    first_left = n_far - left

    def pair_loop(jj, carry):
        far_group(first_left + 2 * jj, 2)
        return carry

    lax.fori_loop(0, left // 2, pair_loop, 0)

    @pl.when(left % 2 == 1)
    def _():
        staged(n_far - 1, 0)


def _attn_scratch(v_rows):
    t = ATTN_TILE
    return [pltpu.VMEM((2, LANES, t), BF16), pltpu.VMEM((2, 1, t), F32),
            pltpu.VMEM((2, v_rows, t), F32), pltpu.VMEM((2, 2, t, t), F32)]


def _diff_kernel(lq1_ref, lk1_ref, lq2_ref, lk2_ref, q_ref, k_ref, v_ref, bias_ref, gain_ref,
                 o_ref, qs, m_s, acc_s, s_scr, *, lambda_init):
    i = pl.program_id(2)
    t = ATTN_TILE
    q = q_ref[0]
    row = lax.broadcasted_iota(jnp.int32, q.shape, 0)
    zero = jnp.zeros_like(q)
    qs[0] = jnp.where(row < HEAD_DIM, q, zero)
    qs[1] = jnp.where(row >= HEAD_DIM, q, zero)
    _init_softmax_state(m_s, acc_s)

    def scores(j, m, which):
        k = k_ref[0, pl.ds(pl.multiple_of(j * t, t), t), :]
        s = jnp.dot(k, qs[m], preferred_element_type=F32)
        return s if which is None else s + bias_ref[which, 0]

    def consume(j, m, st):
        vt = v_ref[0, :, pl.ds(pl.multiple_of(j * t, t), t)]
        _softmax_step(st, vt, m_s, acc_s, m)

    _causal_sweep(i, scores, consume, s_scr)

    lam = (jnp.exp(jnp.sum(lq1_ref[...] * lk1_ref[...], axis=0, keepdims=True))
           - jnp.exp(jnp.sum(lq2_ref[...] * lk2_ref[...], axis=0, keepdims=True)) + lambda_init)
    dv = 2 * HEAD_DIM
    a = _softmax_result(acc_s, 0, dv) - lam * _softmax_result(acc_s, 1, dv)
    y = a * lax.rsqrt(jnp.mean(a * a, axis=0, keepdims=True) + EPS)
    o_ref[0] = (y * gain_ref[...] * (1.0 - lambda_init)).astype(BF16)


def _diff_attention(dq_t, dk, dv_t, bias, lq1, lk1, lq2, lk2, head_gain, lambda_init):
    b, _, s = dq_t.shape
    t = ATTN_TILE
    col = lambda rows: pl.BlockSpec((rows, 1), lambda bi, h, i: (0, 0))
    return pl.pallas_call(
        functools.partial(_diff_kernel, lambda_init=lambda_init),
        grid=(b, N_DIFF_HEADS, s // t),
        in_specs=[col(HEAD_DIM), col(HEAD_DIM), col(HEAD_DIM), col(HEAD_DIM),
                  pl.BlockSpec((1, LANES, t), lambda bi, h, i: (bi, h, i)),
                  pl.BlockSpec((1, s, LANES), lambda bi, h, i: (bi, 0, h)),
                  pl.BlockSpec((1, DIFF_V_ROWS, s), lambda bi, h, i: (bi, h, 0)),
                  pl.BlockSpec((2, 1, t, t), lambda bi, h, i: (0, h, 0, 0)),
                  col(2 * HEAD_DIM)],
        out_specs=pl.BlockSpec((1, LANES, t), lambda bi, h, i: (bi, h, i)),
        out_shape=jax.ShapeDtypeStruct((b, DIFF_COLS, s), BF16),
        scratch_shapes=_attn_scratch(DIFF_V_ROWS),
        compiler_params=_params(3),
        name="diff_attn",
    )(lq1, lk1, lq2, lk2, dq_t, dk, dv_t, bias, head_gain)


def _moba_kernel(q_ref, k_ref, v_ref, km_ref, bias_ref, o_ref, qs, m_s, acc_s, s_scr):
    i = pl.program_id(2)
    t = ATTN_TILE
    row = lax.broadcasted_iota(jnp.int32, (LANES, t), 0)
    lane = lax.broadcasted_iota(jnp.int32, (LANES, t), 1)
    blk = row - HEAD_DIM
    own = i * (t // MOBA_BLOCK) + lane // MOBA_BLOCK
    for hh in range(2):
        qa = q_ref[0, hh * LANES:(hh + 1) * LANES, :]
        km = km_ref[0, hh]
        km_hi = km.astype(BF16)
        km_lo = (km - km_hi.astype(F32)).astype(BF16)
        g = (jnp.dot(km_hi, qa, preferred_element_type=F32)
             + jnp.dot(km_lo, qa, preferred_element_type=F32))
        g = jnp.where((blk >= 0) & (blk < own), g, -jnp.inf)
        sel = blk == own
        for _ in range(MOBA_TOPK):
            mx = jnp.max(g, axis=0, keepdims=True)
            first = jnp.min(jnp.where(g == mx, row, 2 * LANES), axis=0, keepdims=True)
            pick = (row == first) & (mx > -jnp.inf)
            sel = sel | pick
            g = jnp.where(pick, -jnp.inf, g)
        gate = jnp.where(sel, 0.0, NEG).astype(BF16)
        qs[hh] = jnp.where(row < HEAD_DIM, qa, gate)
    _init_softmax_state(m_s, acc_s)

    def scores(j, hh, which):
        k = k_ref[0, pl.ds(pl.multiple_of(j * t, t), t), hh * LANES:(hh + 1) * LANES]
        s = jnp.dot(k, qs[hh], preferred_element_type=F32)
        return s if which is None else s + bias_ref[which, hh]

    def consume(j, hh, st):
        vt = v_ref[0, hh * MOBA_V_ROWS:(hh + 1) * MOBA_V_ROWS, pl.ds(pl.multiple_of(j * t, t), t)]
        _softmax_step(st, vt, m_s, acc_s, hh)

    _causal_sweep(i, scores, consume, s_scr)

    o = jnp.concatenate([_softmax_result(acc_s, 0, HEAD_DIM), _softmax_result(acc_s, 1, HEAD_DIM)],
                        axis=0)
    o_ref[0] = o.astype(BF16)


def _moba_attention(mq_t, mk, mv_t, km, bias):
    b, _, s = mq_t.shape
    t = ATTN_TILE
    return pl.pallas_call(
        _moba_kernel,
        grid=(b, N_MOBA_HEADS // 2, s // t),
        in_specs=[pl.BlockSpec((1, 2 * LANES, t), lambda bi, p, i: (bi, p, i)),
                  pl.BlockSpec((1, s, 2 * LANES), lambda bi, p, i: (bi, 0, p)),
                  pl.BlockSpec((1, 2 * MOBA_V_ROWS, s), lambda bi, p, i: (bi, p, 0)),
                  pl.BlockSpec((1, 2, LANES, LANES), lambda bi, p, i: (bi, p, 0, 0)),
                  pl.BlockSpec((2, 2, t, t), lambda bi, p, i: (0, p, 0, 0))],
        out_specs=pl.BlockSpec((1, LANES, t), lambda bi, p, i: (bi, p, i)),
        out_shape=jax.ShapeDtypeStruct((b, MOBA_COLS, s), BF16),
        scratch_shapes=_attn_scratch(MOBA_V_ROWS),
        compiler_params=_params(3),
        name="moba_attn",
    )(mq_t, mk, mv_t, km, bias)


ROUTER_GROUP_LANE = N_EXPERTS
ROUTE_E1, ROUTE_E2, ROUTE_R1, ROUTE_R2, ROUTE_W1, ROUTE_W2 = range(6)


def _mid_kernel(yd_ref, ym_ref, x_ref, wo_ref, pg_ref, g1_ref, fg_ref, sc2_ref, sh2_ref,
                wr_ref, br_ref, x1_ref, h2_ref, route_ref, tot_ref, run):
    first = (pl.program_id(0) == 0) & (pl.program_id(1) == 0)

    @pl.when(first)
    def _():
        run[...] = jnp.zeros(run.shape, F32)

    tn = (((0,), (0,)), ((), ()))
    y = (lax.dot_general(yd_ref[0], wo_ref[0:DIFF_COLS], tn, preferred_element_type=F32)
         + lax.dot_general(ym_ref[0], wo_ref[DIFF_COLS:D_MODEL], tn, preferred_element_type=F32))
    x1 = x_ref[0] + g1_ref[0] * (_rms(y) * pg_ref[...])
    x1_ref[0] = x1
    h2 = _rms(x1) * fg_ref[...] * (1.0 + sc2_ref[0]) + sh2_ref[0]
    h2_ref[0] = h2

    h_hi = h2.astype(BF16)
    h_lo = (h2 - h_hi.astype(F32)).astype(BF16)
    w_hi = wr_ref[...].astype(BF16)
    w_lo = (wr_ref[...] - w_hi.astype(F32)).astype(BF16)
    logits = (jnp.dot(h_hi, w_hi, preferred_element_type=F32)
              + jnp.dot(h_lo, w_hi, preferred_element_type=F32)
              + jnp.dot(h_hi, w_lo, preferred_element_type=F32)) + br_ref[...]
    lane = lax.broadcasted_iota(jnp.int32, logits.shape, 1)
    is_group = (lane >= ROUTER_GROUP_LANE) & (lane < ROUTER_GROUP_LANE + N_GROUPS)
    gl = jnp.where(is_group, logits, -jnp.inf)
    gmax = jnp.max(gl, axis=1, keepdims=True)
    g_idx = jnp.min(jnp.where(gl == gmax, lane - ROUTER_GROUP_LANE, N_GROUPS), axis=1, keepdims=True)
    g_w = 1.0 / jnp.sum(jnp.exp(gl - gmax), axis=1, keepdims=True)
    in_group = (lane < N_EXPERTS) & (lane // EXPERTS_PER_GROUP == g_idx)
    el = jnp.where(in_group, logits, -jnp.inf)
    m1 = jnp.max(el, axis=1, keepdims=True)
    i1 = jnp.min(jnp.where(el == m1, lane, LANES), axis=1, keepdims=True)
    el2 = jnp.where(lane == i1, -jnp.inf, el)
    m2 = jnp.max(el2, axis=1, keepdims=True)
    i2 = jnp.min(jnp.where(el2 == m2, lane, LANES), axis=1, keepdims=True)
    e2 = jnp.exp(m2 - m1)
    w1 = g_w / (1.0 + e2)
    w2 = g_w * e2 / (1.0 + e2)

    t = MID_TILE
    used = jnp.where((lane == i1) | (lane == i2), 1.0, 0.0)
    earlier = jnp.where(lax.broadcasted_iota(jnp.int32, (t, t), 1)
                        < lax.broadcasted_iota(jnp.int32, (t, t), 0), 1.0, 0.0).astype(BF16)
    rank = jnp.dot(earlier, used.astype(BF16), preferred_element_type=F32) + run[...]
    r1 = jnp.sum(jnp.where(lane == i1, rank, 0.0), axis=1, keepdims=True)
    r2 = jnp.sum(jnp.where(lane == i2, rank, 0.0), axis=1, keepdims=True)
    record = jnp.zeros(logits.shape, F32)
    for field, val in ((ROUTE_E1, i1.astype(F32)), (ROUTE_E2, i2.astype(F32)), (ROUTE_R1, r1),
                       (ROUTE_R2, r2), (ROUTE_W1, w1), (ROUTE_W2, w2)):
        record = jnp.where(lane == field, val, record)
    route_ref[0] = record
    run[...] += jnp.sum(used, axis=0, keepdims=True)
    tot_ref[...] = run[...]


def _mid(yd, ym, x, w_out, post_mix_gain, g1, pre_ffn_gain, sc2, sh2, w_router, b_router):
    b, s, d = x.shape
    t = MID_TILE
    tok = lambda width: pl.BlockSpec((1, t, width), lambda bi, i: (bi, i, 0))
    vec = pl.BlockSpec((1, 1, d), lambda bi, i: (bi, 0, 0))
    row = lambda width: pl.BlockSpec((1, width), lambda bi, i: (0, 0))
    return pl.pallas_call(
        _mid_kernel,
        grid=(b, s // t),
        in_specs=[pl.BlockSpec((1, DIFF_COLS, t), lambda bi, i: (bi, 0, i)),
                  pl.BlockSpec((1, MOBA_COLS, t), lambda bi, i: (bi, 0, i)), tok(d),
                  pl.BlockSpec((d, d), lambda bi, i: (0, 0)),
                  row(d), vec, row(d), vec, vec,
                  pl.BlockSpec((d, LANES), lambda bi, i: (0, 0)), row(LANES)],
        out_specs=[tok(d), tok(d), tok(LANES), row(LANES)],
        out_shape=[jax.ShapeDtypeStruct((b, s, d), F32),
                   jax.ShapeDtypeStruct((b, s, d), F32),
                   jax.ShapeDtypeStruct((b, s, LANES), F32),
                   jax.ShapeDtypeStruct((1, LANES), F32)],
        scratch_shapes=[pltpu.VMEM((1, LANES), F32)],
        compiler_params=_params(2),
        name="out_proj_router",
    )(yd, ym, x, w_out, post_mix_gain, g1, pre_ffn_gain, sc2, sh2, w_router, b_router)


def _row_copy(src, src_row, dst, dst_row, sem):
    return pltpu.make_async_copy(src.at[pl.ds(src_row, 1), :], dst.at[pl.ds(dst_row, 1), :], sem)


def _dispatch_kernel(pos_ref, h_ref, xs_in_ref, xs_ref, sem):
    del xs_in_ref

    def issue(t, carry):
        for k in range(2):
            _row_copy(h_ref, t, xs_ref, pos_ref[0, 0, 2 * t + k], sem).start()
        return carry

    lax.fori_loop(0, DISPATCH_TILE, issue, 0, unroll=ROW_DMA_UNROLL)

    def drain(t, carry):
        for k in range(2):
            _row_copy(h_ref, t, xs_ref, pos_ref[0, 0, 2 * t + k], sem).wait()
        return carry

    lax.fori_loop(0, DISPATCH_TILE, drain, 0, unroll=ROW_DMA_UNROLL)


def _dispatch(pos, h2, xs_zero):
    n, d = h2.shape
    t = DISPATCH_TILE
    return pl.pallas_call(
        _dispatch_kernel,
        grid=(n // t,),
        in_specs=[pl.BlockSpec((1, 1, 2 * t), lambda i: (i, 0, 0), memory_space=pltpu.SMEM),
                  pl.BlockSpec((t, d), lambda i: (i, 0)),
                  pl.BlockSpec(memory_space=pl.ANY)],
        out_specs=pl.BlockSpec(memory_space=pl.ANY),
        out_shape=jax.ShapeDtypeStruct(xs_zero.shape, F32),
        scratch_shapes=[pltpu.SemaphoreType.DMA(())],
        input_output_aliases={2: 0},
        compiler_params=_params(1),
        name="moe_dispatch",
    )(pos.reshape(n // t, 1, 2 * t), h2, xs_zero)


def _expert_kernel(te_ref, x_ref, wg_ref, wu_ref, wd_ref, y_ref, wg_s, wu_s, wd_s):
    j = pl.program_id(0)
    new_expert = (j == 0) | (te_ref[j] != te_ref[jnp.maximum(j - 1, 0)])

    @pl.when(new_expert)
    def _():
        wg_s[...] = wg_ref[0].astype(BF16)
        wu_s[...] = wu_ref[0].astype(BF16)
        wd_s[...] = wd_ref[0].astype(BF16)

    x = x_ref[...].astype(BF16)
    a = jnp.dot(x, wg_s[...], preferred_element_type=F32)
    u = jnp.dot(x, wu_s[...], preferred_element_type=F32)
    hid = (a * jax.nn.sigmoid(a)) * u
    y_ref[...] = jnp.dot(hid.astype(BF16), wd_s[...], preferred_element_type=F32)


def _experts(tile_expert, xs, w_gate, w_up, w_down):
    rows, d = xs.shape
    t = EXPERT_TILE
    return pl.pallas_call(
        _expert_kernel,
        grid_spec=pltpu.PrefetchScalarGridSpec(
            num_scalar_prefetch=1,
            grid=(rows // t,),
            in_specs=[pl.BlockSpec((t, d), lambda j, te: (j, 0)),
                      pl.BlockSpec((1, d, EXPERT_FF), lambda j, te: (te[j], 0, 0)),
                      pl.BlockSpec((1, d, EXPERT_FF), lambda j, te: (te[j], 0, 0)),
                      pl.BlockSpec((1, EXPERT_FF, d), lambda j, te: (te[j], 0, 0))],
            out_specs=pl.BlockSpec((t, d), lambda j, te: (j, 0)),
            scratch_shapes=[pltpu.VMEM((d, EXPERT_FF), BF16), pltpu.VMEM((d, EXPERT_FF), BF16),
                            pltpu.VMEM((EXPERT_FF, d), BF16)]),
        out_shape=jax.ShapeDtypeStruct((rows, d), F32),
        compiler_params=_params(1),
        name="routed_experts",
    )(tile_expert, xs, w_gate, w_up, w_down)


def _combine_kernel(pos_ref, y_ref, route_ref, x1_ref, g2_ref, pg_ref, o_ref, buf, sem):
    t = COMBINE_TILE

    def issue(r, carry):
        for k in range(2):
            _row_copy(y_ref, pos_ref[0, 0, 2 * r + k], buf.at[k], r, sem).start()
        return carry

    lax.fori_loop(0, t, issue, 0, unroll=ROW_DMA_UNROLL)

    def drain(r, carry):
        for k in range(2):
            _row_copy(y_ref, pos_ref[0, 0, 2 * r + k], buf.at[k], r, sem).wait()
        return carry

    lax.fori_loop(0, t, drain, 0, unroll=ROW_DMA_UNROLL)

    route = route_ref[0]
    w1 = route[:, ROUTE_W1:ROUTE_W1 + 1]
    w2 = route[:, ROUTE_W2:ROUTE_W2 + 1]
    y = w1 * buf[0] + w2 * buf[1]
    o_ref[0] = x1_ref[0] + g2_ref[0] * (_rms(y) * pg_ref[...])


def _combine(pos, ys, route, x1, g2, post_ffn_gain):
    b, s, d = x1.shape
    t = COMBINE_TILE
    nt = s // t
    tok = lambda width: pl.BlockSpec((1, t, width), lambda bi, i: (bi, i, 0))
    return pl.pallas_call(
        _combine_kernel,
        grid=(b, nt),
        in_specs=[pl.BlockSpec((1, 1, 2 * t), lambda bi, i: (bi * nt + i, 0, 0),
                               memory_space=pltpu.SMEM),
                  pl.BlockSpec(memory_space=pl.ANY),
                  tok(LANES), tok(d),
                  pl.BlockSpec((1, 1, d), lambda bi, i: (bi, 0, 0)),
                  pl.BlockSpec((1, d), lambda bi, i: (0, 0))],
        out_specs=tok(d),
        out_shape=jax.ShapeDtypeStruct((b, s, d), F32),
        scratch_shapes=[pltpu.VMEM((2, t, d), F32), pltpu.SemaphoreType.DMA(())],
        compiler_params=_params(2),
        name="moe_combine",
    )(pos.reshape(b * nt, 1, 2 * t), ys, route, x1, g2, post_ffn_gain)


def _routing_tables(route, totals, n_rows):
    t = EXPERT_TILE
    tot = totals[0, :N_EXPERTS].astype(jnp.int32)
    padded = (tot + t - 1) // t * t
    ends = jnp.cumsum(padded)
    base = ends - padded
    rec = route.reshape(-1, LANES)
    e = rec[:, ROUTE_E1:ROUTE_E2 + 1].astype(jnp.int32)
    r = rec[:, ROUTE_R1:ROUTE_R2 + 1].astype(jnp.int32)
    pos = base[e] + r
    tile_start = jnp.arange(n_rows // t, dtype=jnp.int32) * t
    tile_expert = jnp.minimum(jnp.sum(tile_start[:, None] >= ends[None, :], axis=1),
                              N_EXPERTS - 1).astype(jnp.int32)
    return pos.reshape(-1), tile_expert


def _augment_w_in(w_in):
    d = w_in.shape[0]
    scale = HEAD_DIM ** -0.5 * LOG2E
    dq, dk, dv, mq, mk, mv = jnp.split(w_in, [512, 1024, 1536, 2048, 2560], axis=1)

    def pad_heads(w):
        w = w.reshape(d, N_MOBA_HEADS, HEAD_DIM)
        return jnp.pad(w, ((0, 0), (0, 0), (0, LANES - HEAD_DIM))).reshape(d, N_MOBA_HEADS * LANES)

    return jnp.concatenate([dq * scale, dk, dv, pad_heads(mq * scale), pad_heads(mk), mv],
                           axis=1).astype(BF16)


def kernel(x, c, w_ada, b_ada, pre_mix_gain, post_mix_gain, pre_ffn_gain, post_ffn_gain, w_in, lambda_q1, lambda_k1, lambda_q2, lambda_k2, diff_head_gain, w_out, rel_bias, w_group, b_group, w_expert, b_expert, w_gate, w_up, w_down):
    b, s, d = x.shape
    depth = w_in.shape[0]
    tab_t = rel_bias.T
    bias_diff = _bias_tiles(tab_t, ATTN_TILE, 0, N_DIFF_HEADS)
    bias_moba = _bias_tiles(tab_t, ATTN_TILE, N_DIFF_HEADS, N_MOBA_HEADS)
    c_pad = jnp.pad(c, ((0, 8 - b), (0, 0)))
    for l in range(depth):
        lambda_init = 0.8 - 0.6 * math.exp(-0.3 * l)
        mod = _ada(c_pad, w_ada[l], b_ada[l][None])[:b]
        sh1, sc1, g1, sh2, sc2, g2 = [m[:, None, :] for m in jnp.split(mod, 6, axis=-1)]

        dq, dk, dv, mq, mk, mv, kmean = _inproj(x, pre_mix_gain[l][None], sc1, sh1,
                                                _augment_w_in(w_in[l]))
        y_diff = _diff_attention(dq, dk, dv, bias_diff, lambda_q1[l][:, None], lambda_k1[l][:, None],
                                 lambda_q2[l][:, None], lambda_k2[l][:, None],
                                 diff_head_gain[l][:, None], lambda_init)
        nkb = s // MOBA_BLOCK
        km = kmean.reshape(b, nkb, N_MOBA_HEADS, LANES)[..., :HEAD_DIM]
        km = jnp.pad(km.transpose(0, 2, 1, 3),
                     ((0, 0), (0, 0), (HEAD_DIM, LANES - HEAD_DIM - nkb), (0, LANES - HEAD_DIM)))
        y_moba = _moba_attention(mq, mk, mv, km, bias_moba)

        w_router = jnp.pad(jnp.concatenate([w_expert[l], w_group[l]], axis=1),
                           ((0, 0), (0, LANES - N_EXPERTS - N_GROUPS)))
        b_router = jnp.pad(jnp.concatenate([b_expert[l], b_group[l]]),
                           (0, LANES - N_EXPERTS - N_GROUPS))[None]
        x1, h2, route, totals = _mid(
            y_diff, y_moba, x, w_out[l].astype(BF16), post_mix_gain[l][None], g1,
            pre_ffn_gain[l][None], sc2, sh2, w_router, b_router)
        n_rows = 2 * b * s + N_EXPERTS * EXPERT_TILE
        pos, tile_expert = _routing_tables(route, totals, n_rows)
        xs = _dispatch(pos, h2.reshape(b * s, d), jnp.zeros((n_rows, d), F32))
        ys = _experts(tile_expert, xs, w_gate[l], w_up[l], w_down[l])
        x = _combine(pos, ys, route, x1, g2, post_ffn_gain[l][None])
    return x
```

```python
import functools
import math

import jax
import jax.numpy as jnp
from jax import lax
from jax.experimental import pallas as pl
from jax.experimental.pallas import tpu as pltpu

F32 = jnp.float32
BF16 = jnp.bfloat16
HIGHEST = lax.Precision.HIGHEST

D_MODEL = 1024
HEAD_DIM = 64
N_DIFF_HEADS = 4
N_MOBA_HEADS = 8
DIFF_COLS = 512
MOBA_COLS = 512
MOBA_BLOCK = 256
MOBA_TOPK = 3
N_BUCKETS = 32
MAX_EXACT = N_BUCKETS // 2
MAX_DISTANCE = 128
N_GROUPS = 4
EXPERTS_PER_GROUP = 8
N_EXPERTS = 32
EXPERT_FF = 512
EPS = 1e-6
NEG = -1e30
LANES = 128
LOG2E = math.log2(math.e)
ONES_ROWS = 16
DIFF_V_ROWS = 2 * HEAD_DIM + ONES_ROWS
MOBA_V_ROWS = HEAD_DIM + ONES_ROWS

IN_TILE = 512
ATTN_TILE = 512
FAR_UNROLL = 8
MID_TILE = 512
DISPATCH_TILE = 512
EXPERT_TILE = 256
COMBINE_TILE = 512
ROW_DMA_UNROLL = 8
VMEM_LIMIT = 56 * 1024 * 1024


def _params(n_axes, vmem=VMEM_LIMIT):
    return pltpu.CompilerParams(dimension_semantics=("arbitrary",) * n_axes,
                                vmem_limit_bytes=vmem)


def _rms(v):
    return v * lax.rsqrt(jnp.mean(v * v, axis=-1, keepdims=True) + EPS)


def _ada_kernel(c_ref, w_ref, b_ref, o_ref):
    c = c_ref[...]
    s = c * jax.nn.sigmoid(c)
    o_ref[...] = jnp.dot(s, w_ref[...], precision=HIGHEST,
                         preferred_element_type=F32) + b_ref[...]


def _ada(c_pad, w_ada, b_ada):
    n = w_ada.shape[1]
    return pl.pallas_call(
        _ada_kernel,
        grid=(n // D_MODEL,),
        in_specs=[pl.BlockSpec((8, D_MODEL), lambda j: (0, 0)),
                  pl.BlockSpec((D_MODEL, D_MODEL), lambda j: (0, j)),
                  pl.BlockSpec((1, D_MODEL), lambda j: (0, j))],
        out_specs=pl.BlockSpec((8, D_MODEL), lambda j: (0, j)),
        out_shape=jax.ShapeDtypeStruct((8, n), F32),
        compiler_params=_params(1),
        name="ada_mod",
    )(c_pad, w_ada, b_ada)


def _bias_kernel(tab_ref, o_ref, *, tile, head0):
    w = pl.program_id(0)
    h = pl.program_id(1) + head0
    r = lax.broadcasted_iota(jnp.int32, (tile, tile), 0)
    c = lax.broadcasted_iota(jnp.int32, (tile, tile), 1)
    d = c - r + w * tile
    n = jnp.maximum(d, 0)
    nf = jnp.maximum(n, 1).astype(F32)
    large = MAX_EXACT + (jnp.log(nf / MAX_EXACT) / math.log(MAX_DISTANCE / MAX_EXACT)
                         * (N_BUCKETS - MAX_EXACT)).astype(jnp.int32)
    large = jnp.minimum(large, N_BUCKETS - 1)
    bucket = jnp.where(n < MAX_EXACT, n, large)
    far = tab_ref[h, N_BUCKETS - 1]
    val = jnp.zeros((tile, tile), F32)
    for b in range(N_BUCKETS - 1):
        val = jnp.where(bucket == b, (tab_ref[h, b] - far) * LOG2E, val)
    o_ref[0, 0] = jnp.where(d >= 0, val, NEG)


def _bias_tiles(tab_t, tile, head0, n_heads):
    return pl.pallas_call(
        functools.partial(_bias_kernel, tile=tile, head0=head0),
        grid=(2, n_heads),
        in_specs=[pl.BlockSpec(memory_space=pltpu.SMEM)],
        out_specs=pl.BlockSpec((1, 1, tile, tile), lambda w, h: (w, h, 0, 0)),
        out_shape=jax.ShapeDtypeStruct((2, n_heads, tile, tile), F32),
        compiler_params=_params(2),
        name="bias_tiles",
    )(tab_t)


W_DQ, W_DK, W_DV = 0, 512, 1024
W_MQ, W_MK, W_MV = 1536, 2560, 3584
W_END = 4096


def _inproj_kernel(x_ref, gain_ref, sc_ref, sh_ref, w_ref,
                   dq_ref, dk_ref, dv_ref, mq_ref, mk_ref, mv_ref, km_ref):
    i = pl.program_id(1)
    h = _rms(x_ref[0]) * gain_ref[...]
    h = h * (1.0 + sc_ref[0]) + sh_ref[0]
    hb = h.astype(BF16)

    def proj(lo, hi):
        return jnp.dot(hb, w_ref[:, lo:hi], preferred_element_type=F32)

    ones = jnp.ones((ONES_ROWS, IN_TILE), F32)

    def with_ones(vt, rows):
        parts = []
        for r0 in range(0, vt.shape[0], rows):
            parts += [vt[r0:r0 + rows], ones]
        return jnp.concatenate(parts, axis=0).astype(BF16)

    dq_ref[0] = proj(W_DQ, W_DK).T.astype(BF16)
    dk_ref[0] = proj(W_DK, W_DV).astype(BF16)
    dv_ref[0] = with_ones(proj(W_DV, W_MQ).T, 2 * HEAD_DIM)
    mq_ref[0] = proj(W_MQ, W_MK).T.astype(BF16)
    mv_ref[0] = with_ones(proj(W_MV, W_END).T, HEAD_DIM)
    mk = proj(W_MK, W_MV)
    for r in range(IN_TILE // MOBA_BLOCK):
        km_ref[0, r] = jnp.mean(mk[r * MOBA_BLOCK:(r + 1) * MOBA_BLOCK], axis=0, keepdims=True)
    row = lax.broadcasted_iota(jnp.int32, mk.shape, 0)
    lane = lax.broadcasted_iota(jnp.int32, mk.shape, 1)
    blk = (i * IN_TILE + row) // MOBA_BLOCK
    onehot = (lane % LANES) - HEAD_DIM == blk
    mk_ref[0] = jnp.where(onehot, 1.0, mk).astype(BF16)


def _inproj(x, gain, sc1, sh1, w_aug):
    b, s, d = x.shape
    nkb = s // MOBA_BLOCK
    tok = lambda width: pl.BlockSpec((1, IN_TILE, width), lambda bi, i: (bi, i, 0))
    tok_t = lambda rows: pl.BlockSpec((1, rows, IN_TILE), lambda bi, i: (bi, 0, i))
    vec = pl.BlockSpec((1, 1, d), lambda bi, i: (bi, 0, 0))
    shp = lambda width: jax.ShapeDtypeStruct((b, s, width), BF16)
    shp_t = lambda rows: jax.ShapeDtypeStruct((b, rows, s), BF16)
    dv_rows = N_DIFF_HEADS * DIFF_V_ROWS
    mv_rows = N_MOBA_HEADS * MOBA_V_ROWS
    return pl.pallas_call(
        _inproj_kernel,
        grid=(b, s // IN_TILE),
        in_specs=[tok(d), pl.BlockSpec((1, d), lambda bi, i: (0, 0)), vec, vec,
                  pl.BlockSpec((d, W_END), lambda bi, i: (0, 0))],
        out_specs=[tok_t(512), tok(512), tok_t(dv_rows), tok_t(1024), tok(1024), tok_t(mv_rows),
                   pl.BlockSpec((1, IN_TILE // MOBA_BLOCK, 1, 1024), lambda bi, i: (bi, i, 0, 0))],
        out_shape=[shp_t(512), shp(512), shp_t(dv_rows), shp_t(1024), shp(1024), shp_t(mv_rows),
                   jax.ShapeDtypeStruct((b, nkb, 1, 1024), F32)],
        compiler_params=_params(2),
        name="in_proj",
    )(x, gain, sc1, sh1, w_aug)


def _softmax_step(st, vt, m_ref, acc_ref, idx):
    m_prev = m_ref[idx]
    m_next = jnp.maximum(m_prev, jnp.max(st, axis=0, keepdims=True))
    alpha = jnp.exp2(m_prev - m_next)
    p = jnp.exp2(st - m_next).astype(BF16)
    acc_ref[idx] = alpha * acc_ref[idx] + jnp.dot(vt, p, preferred_element_type=F32)
    m_ref[idx] = m_next


def _softmax_result(acc_ref, idx, rows):
    acc = acc_ref[idx]
    return acc[:rows] / acc[rows:rows + 1]


def _init_softmax_state(m_s, acc_s):
    m_s[...] = jnp.full(m_s.shape, -jnp.inf, F32)
    acc_s[...] = jnp.zeros(acc_s.shape, F32)


def _causal_sweep(i, scores, consume, s_scr):
    def direct(j, which):
        for m in range(2):
            consume(j, m, scores(j, m, which))

    @pl.when(i == 0)
    def _():
        direct(i, 0)

    @pl.when(i > 0)
    def _():
        direct(i, 0)
        direct(i - 1, 1)

    n_far = jnp.maximum(i - 1, 0)

    def produce(j, slot):
        for m in range(2):
            s_scr[slot, m] = scores(j, m, None)

    def staged(j, slot):
        for m in range(2):
            consume(j, m, s_scr[slot, m])

    @pl.when(n_far > 0)
    def _():
        produce(0, 0)

    def far_group(first, count):
        for u in range(count):
            produce(jnp.minimum(first + u + 1, n_far - 1), (u + 1) % 2)
            staged(first + u, u % 2)

    def far_loop(jj, carry):
        far_group(FAR_UNROLL * jj, FAR_UNROLL)
        return carry

    lax.fori_loop(0, n_far // FAR_UNROLL, far_loop, 0)

    left = n_far % FAR_UNROLL
    first_left = n_far - left

    def pair_loop(jj, carry):
        far_group(first_left + 2 * jj, 2)
        return carry

    lax.fori_loop(0, left // 2, pair_loop, 0)

    @pl.when(left % 2 == 1)
    def _():
        staged(n_far - 1, 0)


def _attn_scratch(v_rows):
    t = ATTN_TILE
    return [pltpu.VMEM((2, LANES, t), BF16), pltpu.VMEM((2, 1, t), F32),
            pltpu.VMEM((2, v_rows, t), F32), pltpu.VMEM((2, 2, t, t), F32)]


def _diff_kernel(lq1_ref, lk1_ref, lq2_ref, lk2_ref, q_ref, k_ref, v_ref, bias_ref, gain_ref,
                 o_ref, qs, m_s, acc_s, s_scr, *, lambda_init):
    i = pl.program_id(2)
    t = ATTN_TILE
    q = q_ref[0]
    row = lax.broadcasted_iota(jnp.int32, q.shape, 0)
    zero = jnp.zeros_like(q)
    qs[0] = jnp.where(row < HEAD_DIM, q, zero)
    qs[1] = jnp.where(row >= HEAD_DIM, q, zero)
    _init_softmax_state(m_s, acc_s)

    def scores(j, m, which):
        k = k_ref[0, pl.ds(pl.multiple_of(j * t, t), t), :]
        s = jnp.dot(k, qs[m], preferred_element_type=F32)
        return s if which is None else s + bias_ref[which, 0]

    def consume(j, m, st):
        vt = v_ref[0, :, pl.ds(pl.multiple_of(j * t, t), t)]
        _softmax_step(st, vt, m_s, acc_s, m)

    _causal_sweep(i, scores, consume, s_scr)

    lam = (jnp.exp(jnp.sum(lq1_ref[...] * lk1_ref[...], axis=0, keepdims=True))
           - jnp.exp(jnp.sum(lq2_ref[...] * lk2_ref[...], axis=0, keepdims=True)) + lambda_init)
    dv = 2 * HEAD_DIM
    a = _softmax_result(acc_s, 0, dv) - lam * _softmax_result(acc_s, 1, dv)
    y = a * lax.rsqrt(jnp.mean(a * a, axis=0, keepdims=True) + EPS)
    o_ref[0] = (y * gain_ref[...] * (1.0 - lambda_init)).astype(BF16)


def _diff_attention(dq_t, dk, dv_t, bias, lq1, lk1, lq2, lk2, head_gain, lambda_init):
    b, _, s = dq_t.shape
    t = ATTN_TILE
    col = lambda rows: pl.BlockSpec((rows, 1), lambda bi, h, i: (0, 0))
    return pl.pallas_call(
        functools.partial(_diff_kernel, lambda_init=lambda_init),
        grid=(b, N_DIFF_HEADS, s // t),
        in_specs=[col(HEAD_DIM), col(HEAD_DIM), col(HEAD_DIM), col(HEAD_DIM),
                  pl.BlockSpec((1, LANES, t), lambda bi, h, i: (bi, h, i)),
                  pl.BlockSpec((1, s, LANES), lambda bi, h, i: (bi, 0, h)),
                  pl.BlockSpec((1, DIFF_V_ROWS, s), lambda bi, h, i: (bi, h, 0)),
                  pl.BlockSpec((2, 1, t, t), lambda bi, h, i: (0, h, 0, 0)),
                  col(2 * HEAD_DIM)],
        out_specs=pl.BlockSpec((1, LANES, t), lambda bi, h, i: (bi, h, i)),
        out_shape=jax.ShapeDtypeStruct((b, DIFF_COLS, s), BF16),
        scratch_shapes=_attn_scratch(DIFF_V_ROWS),
        compiler_params=_params(3),
        name="diff_attn",
    )(lq1, lk1, lq2, lk2, dq_t, dk, dv_t, bias, head_gain)


def _moba_kernel(q_ref, k_ref, v_ref, km_ref, bias_ref, o_ref, qs, m_s, acc_s, s_scr):
    i = pl.program_id(2)
    t = ATTN_TILE
    row = lax.broadcasted_iota(jnp.int32, (LANES, t), 0)
    lane = lax.broadcasted_iota(jnp.int32, (LANES, t), 1)
    blk = row - HEAD_DIM
    own = i * (t // MOBA_BLOCK) + lane // MOBA_BLOCK
    for hh in range(2):
        qa = q_ref[0, hh * LANES:(hh + 1) * LANES, :]
        km = km_ref[0, hh]
        km_hi = km.astype(BF16)
        km_lo = (km - km_hi.astype(F32)).astype(BF16)
        g = (jnp.dot(km_hi, qa, preferred_element_type=F32)
             + jnp.dot(km_lo, qa, preferred_element_type=F32))
        g = jnp.where((blk >= 0) & (blk < own), g, -jnp.inf)
        sel = blk == own
        for _ in range(MOBA_TOPK):
            mx = jnp.max(g, axis=0, keepdims=True)
            first = jnp.min(jnp.where(g == mx, row, 2 * LANES), axis=0, keepdims=True)
            pick = (row == first) & (mx > -jnp.inf)
            sel = sel | pick
            g = jnp.where(pick, -jnp.inf, g)
        gate = jnp.where(sel, 0.0, NEG).astype(BF16)
        qs[hh] = jnp.where(row < HEAD_DIM, qa, gate)
    _init_softmax_state(m_s, acc_s)

    def scores(j, hh, which):
        k = k_ref[0, pl.ds(pl.multiple_of(j * t, t), t), hh * LANES:(hh + 1) * LANES]
        s = jnp.dot(k, qs[hh], preferred_element_type=F32)
        return s if which is None else s + bias_ref[which, hh]

    def consume(j, hh, st):
        vt = v_ref[0, hh * MOBA_V_ROWS:(hh + 1) * MOBA_V_ROWS, pl.ds(pl.multiple_of(j * t, t), t)]
        _softmax_step(st, vt, m_s, acc_s, hh)

    _causal_sweep(i, scores, consume, s_scr)

    o = jnp.concatenate([_softmax_result(acc_s, 0, HEAD_DIM), _softmax_result(acc_s, 1, HEAD_DIM)],
                        axis=0)
    o_ref[0] = o.astype(BF16)


def _moba_attention(mq_t, mk, mv_t, km, bias):
    b, _, s = mq_t.shape
    t = ATTN_TILE
    return pl.pallas_call(
        _moba_kernel,
        grid=(b, N_MOBA_HEADS // 2, s // t),
        in_specs=[pl.BlockSpec((1, 2 * LANES, t), lambda bi, p, i: (bi, p, i)),
                  pl.BlockSpec((1, s, 2 * LANES), lambda bi, p, i: (bi, 0, p)),
                  pl.BlockSpec((1, 2 * MOBA_V_ROWS, s), lambda bi, p, i: (bi, p, 0)),
                  pl.BlockSpec((1, 2, LANES, LANES), lambda bi, p, i: (bi, p, 0, 0)),
                  pl.BlockSpec((2, 2, t, t), lambda bi, p, i: (0, p, 0, 0))],
        out_specs=pl.BlockSpec((1, LANES, t), lambda bi, p, i: (bi, p, i)),
        out_shape=jax.ShapeDtypeStruct((b, MOBA_COLS, s), BF16),
        scratch_shapes=_attn_scratch(MOBA_V_ROWS),
        compiler_params=_params(3),
        name="moba_attn",
    )(mq_t, mk, mv_t, km, bias)


ROUTER_GROUP_LANE = N_EXPERTS
ROUTE_E1, ROUTE_E2, ROUTE_R1, ROUTE_R2, ROUTE_W1, ROUTE_W2 = range(6)


def _mid_kernel(yd_ref, ym_ref, x_ref, wo_ref, pg_ref, g1_ref, fg_ref, sc2_ref, sh2_ref,
                wr_ref, br_ref, x1_ref, h2_ref, route_ref, tot_ref, run):
    first = (pl.program_id(0) == 0) & (pl.program_id(1) == 0)

    @pl.when(first)
    def _():
        run[...] = jnp.zeros(run.shape, F32)

    tn = (((0,), (0,)), ((), ()))
    y = (lax.dot_general(yd_ref[0], wo_ref[0:DIFF_COLS], tn, preferred_element_type=F32)
         + lax.dot_general(ym_ref[0], wo_ref[DIFF_COLS:D_MODEL], tn, preferred_element_type=F32))
    x1 = x_ref[0] + g1_ref[0] * (_rms(y) * pg_ref[...])
    x1_ref[0] = x1
    h2 = _rms(x1) * fg_ref[...] * (1.0 + sc2_ref[0]) + sh2_ref[0]
    h2_ref[0] = h2

    h_hi = h2.astype(BF16)
    h_lo = (h2 - h_hi.astype(F32)).astype(BF16)
    w_hi = wr_ref[...].astype(BF16)
    w_lo = (wr_ref[...] - w_hi.astype(F32)).astype(BF16)
    logits = (jnp.dot(h_hi, w_hi, preferred_element_type=F32)
              + jnp.dot(h_lo, w_hi, preferred_element_type=F32)
              + jnp.dot(h_hi, w_lo, preferred_element_type=F32)) + br_ref[...]
    lane = lax.broadcasted_iota(jnp.int32, logits.shape, 1)
    is_group = (lane >= ROUTER_GROUP_LANE) & (lane < ROUTER_GROUP_LANE + N_GROUPS)
    gl = jnp.where(is_group, logits, -jnp.inf)
    gmax = jnp.max(gl, axis=1, keepdims=True)
    g_idx = jnp.min(jnp.where(gl == gmax, lane - ROUTER_GROUP_LANE, N_GROUPS), axis=1, keepdims=True)
    g_w = 1.0 / jnp.sum(jnp.exp(gl - gmax), axis=1, keepdims=True)
    in_group = (lane < N_EXPERTS) & (lane // EXPERTS_PER_GROUP == g_idx)
    el = jnp.where(in_group, logits, -jnp.inf)
    m1 = jnp.max(el, axis=1, keepdims=True)
    i1 = jnp.min(jnp.where(el == m1, lane, LANES), axis=1, keepdims=True)
    el2 = jnp.where(lane == i1, -jnp.inf, el)
    m2 = jnp.max(el2, axis=1, keepdims=True)
    i2 = jnp.min(jnp.where(el2 == m2, lane, LANES), axis=1, keepdims=True)
    e2 = jnp.exp(m2 - m1)
    w1 = g_w / (1.0 + e2)
    w2 = g_w * e2 / (1.0 + e2)

    t = MID_TILE
    used = jnp.where((lane == i1) | (lane == i2), 1.0, 0.0)
    earlier = jnp.where(lax.broadcasted_iota(jnp.int32, (t, t), 1)
                        < lax.broadcasted_iota(jnp.int32, (t, t), 0), 1.0, 0.0).astype(BF16)
    rank = jnp.dot(earlier, used.astype(BF16), preferred_element_type=F32) + run[...]
    r1 = jnp.sum(jnp.where(lane == i1, rank, 0.0), axis=1, keepdims=True)
    r2 = jnp.sum(jnp.where(lane == i2, rank, 0.0), axis=1, keepdims=True)
    record = jnp.zeros(logits.shape, F32)
    for field, val in ((ROUTE_E1, i1.astype(F32)), (ROUTE_E2, i2.astype(F32)), (ROUTE_R1, r1),
                       (ROUTE_R2, r2), (ROUTE_W1, w1), (ROUTE_W2, w2)):
        record = jnp.where(lane == field, val, record)
    route_ref[0] = record
    run[...] += jnp.sum(used, axis=0, keepdims=True)
    tot_ref[...] = run[...]


def _mid(yd, ym, x, w_out, post_mix_gain, g1, pre_ffn_gain, sc2, sh2, w_router, b_router):
    b, s, d = x.shape
    t = MID_TILE
    tok = lambda width: pl.BlockSpec((1, t, width), lambda bi, i: (bi, i, 0))
    vec = pl.BlockSpec((1, 1, d), lambda bi, i: (bi, 0, 0))
    row = lambda width: pl.BlockSpec((1, width), lambda bi, i: (0, 0))
    return pl.pallas_call(
        _mid_kernel,
        grid=(b, s // t),
        in_specs=[pl.BlockSpec((1, DIFF_COLS, t), lambda bi, i: (bi, 0, i)),
                  pl.BlockSpec((1, MOBA_COLS, t), lambda bi, i: (bi, 0, i)), tok(d),
                  pl.BlockSpec((d, d), lambda bi, i: (0, 0)),
                  row(d), vec, row(d), vec, vec,
                  pl.BlockSpec((d, LANES), lambda bi, i: (0, 0)), row(LANES)],
        out_specs=[tok(d), tok(d), tok(LANES), row(LANES)],
        out_shape=[jax.ShapeDtypeStruct((b, s, d), F32),
                   jax.ShapeDtypeStruct((b, s, d), F32),
                   jax.ShapeDtypeStruct((b, s, LANES), F32),
                   jax.ShapeDtypeStruct((1, LANES), F32)],
        scratch_shapes=[pltpu.VMEM((1, LANES), F32)],
        compiler_params=_params(2),
        name="out_proj_router",
    )(yd, ym, x, w_out, post_mix_gain, g1, pre_ffn_gain, sc2, sh2, w_router, b_router)


def _row_copy(src, src_row, dst, dst_row, sem):
    return pltpu.make_async_copy(src.at[pl.ds(src_row, 1), :], dst.at[pl.ds(dst_row, 1), :], sem)


def _dispatch_kernel(pos_ref, ends_ref, h_ref, xs_ref, zeros, sem, zero_sem):
    @pl.when(pl.program_id(0) == 0)
    def _():
        zeros[...] = jnp.zeros(zeros.shape, F32)

        def fill(e):
            start = pl.multiple_of(ends_ref[0, e] - EXPERT_TILE, EXPERT_TILE)
            return pltpu.make_async_copy(zeros, xs_ref.at[pl.ds(start, EXPERT_TILE), :], zero_sem)

        def has_rows(e):
            return ends_ref[0, e] > jnp.where(e == 0, 0, ends_ref[0, jnp.maximum(e - 1, 0)])

        def fill_start(e, carry):
            @pl.when(has_rows(e))
            def _():
                fill(e).start()
            return carry

        lax.fori_loop(0, N_EXPERTS, fill_start, 0)

        def fill_wait(e, carry):
            @pl.when(has_rows(e))
            def _():
                fill(e).wait()
            return carry

        lax.fori_loop(0, N_EXPERTS, fill_wait, 0)

        n_used = ends_ref[0, N_EXPERTS - 1] // EXPERT_TILE
        n_tiles = xs_ref.shape[0] // EXPERT_TILE

        def spare(j):
            start = pl.multiple_of(j * EXPERT_TILE, EXPERT_TILE)
            return pltpu.make_async_copy(zeros, xs_ref.at[pl.ds(start, EXPERT_TILE), :], zero_sem)

        def spare_start(j, carry):
            spare(j).start()
            return carry

        lax.fori_loop(n_used, n_tiles, spare_start, 0)

        def spare_wait(j, carry):
            spare(j).wait()
            return carry

        lax.fori_loop(n_used, n_tiles, spare_wait, 0)

    def issue(t, carry):
        for k in range(2):
            _row_copy(h_ref, t, xs_ref, pos_ref[0, 0, 2 * t + k], sem).start()
        return carry

    lax.fori_loop(0, DISPATCH_TILE, issue, 0, unroll=ROW_DMA_UNROLL)

    def drain(t, carry):
        for k in range(2):
            _row_copy(h_ref, t, xs_ref, pos_ref[0, 0, 2 * t + k], sem).wait()
        return carry

    lax.fori_loop(0, DISPATCH_TILE, drain, 0, unroll=ROW_DMA_UNROLL)


def _dispatch(pos, ends, h2, n_rows):
    n, d = h2.shape
    t = DISPATCH_TILE
    return pl.pallas_call(
        _dispatch_kernel,
        grid=(n // t,),
        in_specs=[pl.BlockSpec((1, 1, 2 * t), lambda i: (i, 0, 0), memory_space=pltpu.SMEM),
                  pl.BlockSpec(memory_space=pltpu.SMEM),
                  pl.BlockSpec((t, d), lambda i: (i, 0))],
        out_specs=pl.BlockSpec(memory_space=pl.ANY),
        out_shape=jax.ShapeDtypeStruct((n_rows, d), F32),
        scratch_shapes=[pltpu.VMEM((EXPERT_TILE, d), F32), pltpu.SemaphoreType.DMA(()),
                        pltpu.SemaphoreType.DMA(())],
        compiler_params=_params(1),
        name="moe_dispatch",
    )(pos.reshape(n // t, 1, 2 * t), ends[None], h2)


def _expert_kernel(te_ref, x_ref, wg_ref, wu_ref, wd_ref, y_ref, wg_s, wu_s, wd_s):
    j = pl.program_id(0)
    n_used = te_ref[pl.num_programs(0)]

    @pl.when(j >= n_used)
    def _():
        y_ref[...] = jnp.zeros(y_ref.shape, F32)

    @pl.when(j < n_used)
    def _():
        new_expert = (j == 0) | (te_ref[j] != te_ref[jnp.maximum(j - 1, 0)])

        @pl.when(new_expert)
        def _():
            wg_s[...] = wg_ref[0].astype(BF16)
            wu_s[...] = wu_ref[0].astype(BF16)
            wd_s[...] = wd_ref[0].astype(BF16)

        x = x_ref[...].astype(BF16)
        a = jnp.dot(x, wg_s[...], preferred_element_type=F32)
        u = jnp.dot(x, wu_s[...], preferred_element_type=F32)
        hid = (a * jax.nn.sigmoid(a)) * u
        y_ref[...] = jnp.dot(hid.astype(BF16), wd_s[...], preferred_element_type=F32)


def _experts(tile_expert, xs, w_gate, w_up, w_down):
    rows, d = xs.shape
    t = EXPERT_TILE
    n_tiles = rows // t
    return pl.pallas_call(
        _expert_kernel,
        grid_spec=pltpu.PrefetchScalarGridSpec(
            num_scalar_prefetch=1,
            grid=(n_tiles,),
            in_specs=[pl.BlockSpec((t, d), lambda j, te: (jnp.minimum(j, te[n_tiles] - 1), 0)),
                      pl.BlockSpec((1, d, EXPERT_FF), lambda j, te: (te[j], 0, 0)),
                      pl.BlockSpec((1, d, EXPERT_FF), lambda j, te: (te[j], 0, 0)),
                      pl.BlockSpec((1, EXPERT_FF, d), lambda j, te: (te[j], 0, 0))],
            out_specs=pl.BlockSpec((t, d), lambda j, te: (j, 0)),
            scratch_shapes=[pltpu.VMEM((d, EXPERT_FF), BF16), pltpu.VMEM((d, EXPERT_FF), BF16),
                            pltpu.VMEM((EXPERT_FF, d), BF16)]),
        out_shape=jax.ShapeDtypeStruct((rows, d), F32),
        compiler_params=_params(1),
        name="routed_experts",
    )(tile_expert, xs, w_gate, w_up, w_down)


def _combine_kernel(pos_ref, y_ref, route_ref, x1_ref, g2_ref, pg_ref, o_ref, buf, sem):
    t = COMBINE_TILE

    def issue(r, carry):
        for k in range(2):
            _row_copy(y_ref, pos_ref[0, 0, 2 * r + k], buf.at[k], r, sem).start()
        return carry

    lax.fori_loop(0, t, issue, 0, unroll=ROW_DMA_UNROLL)

    def drain(r, carry):
        for k in range(2):
            _row_copy(y_ref, pos_ref[0, 0, 2 * r + k], buf.at[k], r, sem).wait()
        return carry

    lax.fori_loop(0, t, drain, 0, unroll=ROW_DMA_UNROLL)

    route = route_ref[0]
    w1 = route[:, ROUTE_W1:ROUTE_W1 + 1]
    w2 = route[:, ROUTE_W2:ROUTE_W2 + 1]
    y = w1 * buf[0] + w2 * buf[1]
    o_ref[0] = x1_ref[0] + g2_ref[0] * (_rms(y) * pg_ref[...])


def _combine(pos, ys, route, x1, g2, post_ffn_gain):
    b, s, d = x1.shape
    t = COMBINE_TILE
    nt = s // t
    tok = lambda width: pl.BlockSpec((1, t, width), lambda bi, i: (bi, i, 0))
    return pl.pallas_call(
        _combine_kernel,
        grid=(b, nt),
        in_specs=[pl.BlockSpec((1, 1, 2 * t), lambda bi, i: (bi * nt + i, 0, 0),
                               memory_space=pltpu.SMEM),
                  pl.BlockSpec(memory_space=pl.ANY),
                  tok(LANES), tok(d),
                  pl.BlockSpec((1, 1, d), lambda bi, i: (bi, 0, 0)),
                  pl.BlockSpec((1, d), lambda bi, i: (0, 0))],
        out_specs=tok(d),
        out_shape=jax.ShapeDtypeStruct((b, s, d), F32),
        scratch_shapes=[pltpu.VMEM((2, t, d), F32), pltpu.SemaphoreType.DMA(())],
        compiler_params=_params(2),
        name="moe_combine",
    )(pos.reshape(b * nt, 1, 2 * t), ys, route, x1, g2, post_ffn_gain)


def _routing_tables(route, totals, n_rows):
    t = EXPERT_TILE
    tot = totals[0, :N_EXPERTS].astype(jnp.int32)
    padded = (tot + t - 1) // t * t
    ends = jnp.cumsum(padded)
    base = ends - padded
    rec = route.reshape(-1, LANES)
    e = rec[:, ROUTE_E1:ROUTE_E2 + 1].astype(jnp.int32)
    r = rec[:, ROUTE_R1:ROUTE_R2 + 1].astype(jnp.int32)
    pos = base[e] + r
    tile_start = jnp.arange(n_rows // t, dtype=jnp.int32) * t
    tile_expert = jnp.minimum(jnp.sum(tile_start[:, None] >= ends[None, :], axis=1),
                              N_EXPERTS - 1).astype(jnp.int32)
    tiles_used = ends[-1:] // t
    return pos.reshape(-1), ends, jnp.concatenate([tile_expert, tiles_used])


def _augment_w_in(w_in):
    d = w_in.shape[0]
    scale = HEAD_DIM ** -0.5 * LOG2E
    dq, dk, dv, mq, mk, mv = jnp.split(w_in, [512, 1024, 1536, 2048, 2560], axis=1)

    def pad_heads(w):
        w = w.reshape(d, N_MOBA_HEADS, HEAD_DIM)
        return jnp.pad(w, ((0, 0), (0, 0), (0, LANES - HEAD_DIM))).reshape(d, N_MOBA_HEADS * LANES)

    return jnp.concatenate([dq * scale, dk, dv, pad_heads(mq * scale), pad_heads(mk), mv],
                           axis=1).astype(BF16)


def kernel(x, c, w_ada, b_ada, pre_mix_gain, post_mix_gain, pre_ffn_gain, post_ffn_gain, w_in, lambda_q1, lambda_k1, lambda_q2, lambda_k2, diff_head_gain, w_out, rel_bias, w_group, b_group, w_expert, b_expert, w_gate, w_up, w_down):
    b, s, d = x.shape
    depth = w_in.shape[0]
    tab_t = rel_bias.T
    bias_diff = _bias_tiles(tab_t, ATTN_TILE, 0, N_DIFF_HEADS)
    bias_moba = _bias_tiles(tab_t, ATTN_TILE, N_DIFF_HEADS, N_MOBA_HEADS)
    c_pad = jnp.pad(c, ((0, 8 - b), (0, 0)))
    for l in range(depth):
        lambda_init = 0.8 - 0.6 * math.exp(-0.3 * l)
        mod = _ada(c_pad, w_ada[l], b_ada[l][None])[:b]
        sh1, sc1, g1, sh2, sc2, g2 = [m[:, None, :] for m in jnp.split(mod, 6, axis=-1)]

        dq, dk, dv, mq, mk, mv, kmean = _inproj(x, pre_mix_gain[l][None], sc1, sh1,
                                                _augment_w_in(w_in[l]))
        y_diff = _diff_attention(dq, dk, dv, bias_diff, lambda_q1[l][:, None], lambda_k1[l][:, None],
                                 lambda_q2[l][:, None], lambda_k2[l][:, None],
                                 diff_head_gain[l][:, None], lambda_init)
        nkb = s // MOBA_BLOCK
        km = kmean.reshape(b, nkb, N_MOBA_HEADS, LANES)[..., :HEAD_DIM]
        km = jnp.pad(km.transpose(0, 2, 1, 3),
                     ((0, 0), (0, 0), (HEAD_DIM, LANES - HEAD_DIM - nkb), (0, LANES - HEAD_DIM)))
        y_moba = _moba_attention(mq, mk, mv, km, bias_moba)

        w_router = jnp.pad(jnp.concatenate([w_expert[l], w_group[l]], axis=1),
                           ((0, 0), (0, LANES - N_EXPERTS - N_GROUPS)))
        b_router = jnp.pad(jnp.concatenate([b_expert[l], b_group[l]]),
                           (0, LANES - N_EXPERTS - N_GROUPS))[None]
        x1, h2, route, totals = _mid(
            y_diff, y_moba, x, w_out[l].astype(BF16), post_mix_gain[l][None], g1,
            pre_ffn_gain[l][None], sc2, sh2, w_router, b_router)
        n_rows = 2 * b * s + N_EXPERTS * EXPERT_TILE
        pos, ends, tile_expert = _routing_tables(route, totals, n_rows)
        xs = _dispatch(pos, ends, h2.reshape(b * s, d), n_rows)
        ys = _experts(tile_expert, xs, w_gate[l], w_up[l], w_down[l])
        x = _combine(pos, ys, route, x1, g2, post_ffn_gain[l][None])
    return x
```

```python
import functools
import math

import jax
import jax.numpy as jnp
from jax import lax
from jax.experimental import pallas as pl
from jax.experimental.pallas import tpu as pltpu

F32 = jnp.float32
BF16 = jnp.bfloat16
HIGHEST = lax.Precision.HIGHEST

D_MODEL = 1024
HEAD_DIM = 64
N_DIFF_HEADS = 4
N_MOBA_HEADS = 8
DIFF_COLS = 512
MOBA_COLS = 512
MOBA_BLOCK = 256
MOBA_TOPK = 3
N_BUCKETS = 32
MAX_EXACT = N_BUCKETS // 2
MAX_DISTANCE = 128
N_GROUPS = 4
EXPERTS_PER_GROUP = 8
N_EXPERTS = 32
EXPERT_FF = 512
EPS = 1e-6
NEG = -1e30
LANES = 128
LOG2E = math.log2(math.e)
ONES_ROWS = 16
DIFF_V_ROWS = 2 * HEAD_DIM + ONES_ROWS
MOBA_V_ROWS = HEAD_DIM + ONES_ROWS

IN_TILE = 512
ATTN_TILE = 512
FAR_UNROLL = 8
MID_TILE = 512
DISPATCH_TILE = 512
EXPERT_TILE = 256
COMBINE_TILE = 512
ROW_DMA_UNROLL = 8
VMEM_LIMIT = 56 * 1024 * 1024


def _params(n_axes, vmem=VMEM_LIMIT):
    return pltpu.CompilerParams(dimension_semantics=("arbitrary",) * n_axes,
                                vmem_limit_bytes=vmem)


def _rms(v):
    return v * lax.rsqrt(jnp.mean(v * v, axis=-1, keepdims=True) + EPS)


def _ada_kernel(c_ref, w_ref, b_ref, o_ref):
    c = c_ref[...]
    s = c * jax.nn.sigmoid(c)
    o_ref[...] = jnp.dot(s, w_ref[...], precision=HIGHEST,
                         preferred_element_type=F32) + b_ref[...]


def _ada(c_pad, w_ada, b_ada):
    n = w_ada.shape[1]
    return pl.pallas_call(
        _ada_kernel,
        grid=(n // D_MODEL,),
        in_specs=[pl.BlockSpec((8, D_MODEL), lambda j: (0, 0)),
                  pl.BlockSpec((D_MODEL, D_MODEL), lambda j: (0, j)),
                  pl.BlockSpec((1, D_MODEL), lambda j: (0, j))],
        out_specs=pl.BlockSpec((8, D_MODEL), lambda j: (0, j)),
        out_shape=jax.ShapeDtypeStruct((8, n), F32),
        compiler_params=_params(1),
        name="ada_mod",
    )(c_pad, w_ada, b_ada)


def _bias_kernel(tab_ref, o_ref, *, tile, head0):
    w = pl.program_id(0)
    h = pl.program_id(1) + head0
    r = lax.broadcasted_iota(jnp.int32, (tile, tile), 0)
    c = lax.broadcasted_iota(jnp.int32, (tile, tile), 1)
    d = c - r + w * tile
    n = jnp.maximum(d, 0)
    nf = jnp.maximum(n, 1).astype(F32)
    large = MAX_EXACT + (jnp.log(nf / MAX_EXACT) / math.log(MAX_DISTANCE / MAX_EXACT)
                         * (N_BUCKETS - MAX_EXACT)).astype(jnp.int32)
    large = jnp.minimum(large, N_BUCKETS - 1)
    bucket = jnp.where(n < MAX_EXACT, n, large)
    far = tab_ref[h, N_BUCKETS - 1]
    val = jnp.zeros((tile, tile), F32)
    for b in range(N_BUCKETS - 1):
        val = jnp.where(bucket == b, (tab_ref[h, b] - far) * LOG2E, val)
    o_ref[0, 0] = jnp.where(d >= 0, val, NEG)


def _bias_tiles(tab_t, tile, head0, n_heads):
    return pl.pallas_call(
        functools.partial(_bias_kernel, tile=tile, head0=head0),
        grid=(2, n_heads),
        in_specs=[pl.BlockSpec(memory_space=pltpu.SMEM)],
        out_specs=pl.BlockSpec((1, 1, tile, tile), lambda w, h: (w, h, 0, 0)),
        out_shape=jax.ShapeDtypeStruct((2, n_heads, tile, tile), F32),
        compiler_params=_params(2),
        name="bias_tiles",
    )(tab_t)


W_DQ, W_DK, W_DV = 0, 512, 1024
W_MQ, W_MK, W_MV = 1536, 2560, 3584
W_END = 4096


def _inproj_kernel(x_ref, gain_ref, sc_ref, sh_ref, w_ref,
                   dq_ref, dk_ref, dv_ref, mq_ref, mk_ref, mv_ref, km_ref):
    i = pl.program_id(1)
    h = _rms(x_ref[0]) * gain_ref[...]
    h = h * (1.0 + sc_ref[0]) + sh_ref[0]
    hb = h.astype(BF16)

    def proj(lo, hi):
        return jnp.dot(hb, w_ref[:, lo:hi], preferred_element_type=F32)

    ones = jnp.ones((ONES_ROWS, IN_TILE), F32)

    def with_ones(vt, rows):
        parts = []
        for r0 in range(0, vt.shape[0], rows):
            parts += [vt[r0:r0 + rows], ones]
        return jnp.concatenate(parts, axis=0).astype(BF16)

    dq_ref[0] = proj(W_DQ, W_DK).T.astype(BF16)
    dk_ref[0] = proj(W_DK, W_DV).astype(BF16)
    dv_ref[0] = with_ones(proj(W_DV, W_MQ).T, 2 * HEAD_DIM)
    mq_ref[0] = proj(W_MQ, W_MK).T.astype(BF16)
    mv_ref[0] = with_ones(proj(W_MV, W_END).T, HEAD_DIM)
    mk = proj(W_MK, W_MV)
    for r in range(IN_TILE // MOBA_BLOCK):
        km_ref[0, r] = jnp.mean(mk[r * MOBA_BLOCK:(r + 1) * MOBA_BLOCK], axis=0, keepdims=True)
    row = lax.broadcasted_iota(jnp.int32, mk.shape, 0)
    lane = lax.broadcasted_iota(jnp.int32, mk.shape, 1)
    blk = (i * IN_TILE + row) // MOBA_BLOCK
    onehot = (lane % LANES) - HEAD_DIM == blk
    mk_ref[0] = jnp.where(onehot, 1.0, mk).astype(BF16)


def _inproj(x, gain, sc1, sh1, w_aug):
    b, s, d = x.shape
    nkb = s // MOBA_BLOCK
    tok = lambda width: pl.BlockSpec((1, IN_TILE, width), lambda bi, i: (bi, i, 0))
    tok_t = lambda rows: pl.BlockSpec((1, rows, IN_TILE), lambda bi, i: (bi, 0, i))
    vec = pl.BlockSpec((1, 1, d), lambda bi, i: (bi, 0, 0))
    shp = lambda width: jax.ShapeDtypeStruct((b, s, width), BF16)
    shp_t = lambda rows: jax.ShapeDtypeStruct((b, rows, s), BF16)
    dv_rows = N_DIFF_HEADS * DIFF_V_ROWS
    mv_rows = N_MOBA_HEADS * MOBA_V_ROWS
    return pl.pallas_call(
        _inproj_kernel,
        grid=(b, s // IN_TILE),
        in_specs=[tok(d), pl.BlockSpec((1, d), lambda bi, i: (0, 0)), vec, vec,
                  pl.BlockSpec((d, W_END), lambda bi, i: (0, 0))],
        out_specs=[tok_t(512), tok(512), tok_t(dv_rows), tok_t(1024), tok(1024), tok_t(mv_rows),
                   pl.BlockSpec((1, IN_TILE // MOBA_BLOCK, 1, 1024), lambda bi, i: (bi, i, 0, 0))],
        out_shape=[shp_t(512), shp(512), shp_t(dv_rows), shp_t(1024), shp(1024), shp_t(mv_rows),
                   jax.ShapeDtypeStruct((b, nkb, 1, 1024), F32)],
        compiler_params=_params(2),
        name="in_proj",
    )(x, gain, sc1, sh1, w_aug)


def _softmax_step(st, vt, m_ref, acc_ref, idx):
    m_prev = m_ref[idx]
    m_next = jnp.maximum(m_prev, jnp.max(st, axis=0, keepdims=True))
    alpha = jnp.exp2(m_prev - m_next)
    p = jnp.exp2((st - m_next).astype(BF16))
    acc_ref[idx] = alpha * acc_ref[idx] + jnp.dot(vt, p, preferred_element_type=F32)
    m_ref[idx] = m_next


def _softmax_result(acc_ref, idx, rows):
    acc = acc_ref[idx]
    return acc[:rows] / acc[rows:rows + 1]


def _init_softmax_state(m_s, acc_s):
    m_s[...] = jnp.full(m_s.shape, -jnp.inf, F32)
    acc_s[...] = jnp.zeros(acc_s.shape, F32)


def _causal_sweep(i, scores, consume, s_scr):
    def direct(j, which):
        for m in range(2):
            consume(j, m, scores(j, m, which))

    @pl.when(i == 0)
    def _():
        direct(i, 0)

    @pl.when(i > 0)
    def _():
        direct(i, 0)
        direct(i - 1, 1)

    n_far = jnp.maximum(i - 1, 0)

    def produce(j, slot):
        for m in range(2):
            s_scr[slot, m] = scores(j, m, None)

    def staged(j, slot):
        for m in range(2):
            consume(j, m, s_scr[slot, m])

    @pl.when(n_far > 0)
    def _():
        produce(0, 0)

    def far_group(first, count):
        for u in range(count):
            produce(jnp.minimum(first + u + 1, n_far - 1), (u + 1) % 2)
            staged(first + u, u % 2)

    def far_loop(jj, carry):
        far_group(FAR_UNROLL * jj, FAR_UNROLL)
        return carry

    lax.fori_loop(0, n_far // FAR_UNROLL, far_loop, 0)

    left = n_far % FAR_UNROLL
    first_left = n_far - left

    def pair_loop(jj, carry):
        far_group(first_left + 2 * jj, 2)
        return carry

    lax.fori_loop(0, left // 2, pair_loop, 0)

    @pl.when(left % 2 == 1)
    def _():
        staged(n_far - 1, 0)


def _attn_scratch(v_rows):
    t = ATTN_TILE
    return [pltpu.VMEM((2, LANES, t), BF16), pltpu.VMEM((2, 1, t), F32),
            pltpu.VMEM((2, v_rows, t), F32), pltpu.VMEM((2, 2, t, t), F32)]


def _diff_kernel(lq1_ref, lk1_ref, lq2_ref, lk2_ref, q_ref, k_ref, v_ref, bias_ref, gain_ref,
                 o_ref, qs, m_s, acc_s, s_scr, *, lambda_init):
    i = pl.program_id(2)
    t = ATTN_TILE
    q = q_ref[0]
    row = lax.broadcasted_iota(jnp.int32, q.shape, 0)
    zero = jnp.zeros_like(q)
    qs[0] = jnp.where(row < HEAD_DIM, q, zero)
    qs[1] = jnp.where(row >= HEAD_DIM, q, zero)
    _init_softmax_state(m_s, acc_s)

    def scores(j, m, which):
        k = k_ref[0, pl.ds(pl.multiple_of(j * t, t), t), :]
        s = jnp.dot(k, qs[m], preferred_element_type=F32)
        return s if which is None else s + bias_ref[which, 0]

    def consume(j, m, st):
        vt = v_ref[0, :, pl.ds(pl.multiple_of(j * t, t), t)]
        _softmax_step(st, vt, m_s, acc_s, m)

    _causal_sweep(i, scores, consume, s_scr)

    lam = (jnp.exp(jnp.sum(lq1_ref[...] * lk1_ref[...], axis=0, keepdims=True))
           - jnp.exp(jnp.sum(lq2_ref[...] * lk2_ref[...], axis=0, keepdims=True)) + lambda_init)
    dv = 2 * HEAD_DIM
    a = _softmax_result(acc_s, 0, dv) - lam * _softmax_result(acc_s, 1, dv)
    y = a * lax.rsqrt(jnp.mean(a * a, axis=0, keepdims=True) + EPS)
    o_ref[0] = (y * gain_ref[...] * (1.0 - lambda_init)).astype(BF16)


def _diff_attention(dq_t, dk, dv_t, bias, lq1, lk1, lq2, lk2, head_gain, lambda_init):
    b, _, s = dq_t.shape
    t = ATTN_TILE
    col = lambda rows: pl.BlockSpec((rows, 1), lambda bi, h, i: (0, 0))
    return pl.pallas_call(
        functools.partial(_diff_kernel, lambda_init=lambda_init),
        grid=(b, N_DIFF_HEADS, s // t),
        in_specs=[col(HEAD_DIM), col(HEAD_DIM), col(HEAD_DIM), col(HEAD_DIM),
                  pl.BlockSpec((1, LANES, t), lambda bi, h, i: (bi, h, i)),
                  pl.BlockSpec((1, s, LANES), lambda bi, h, i: (bi, 0, h)),
                  pl.BlockSpec((1, DIFF_V_ROWS, s), lambda bi, h, i: (bi, h, 0)),
                  pl.BlockSpec((2, 1, t, t), lambda bi, h, i: (0, h, 0, 0)),
                  col(2 * HEAD_DIM)],
        out_specs=pl.BlockSpec((1, LANES, t), lambda bi, h, i: (bi, h, i)),
        out_shape=jax.ShapeDtypeStruct((b, DIFF_COLS, s), BF16),
        scratch_shapes=_attn_scratch(DIFF_V_ROWS),
        compiler_params=_params(3),
        name="diff_attn",
    )(lq1, lk1, lq2, lk2, dq_t, dk, dv_t, bias, head_gain)


def _moba_kernel(q_ref, k_ref, v_ref, km_ref, bias_ref, o_ref, qs, m_s, acc_s, s_scr):
    i = pl.program_id(2)
    t = ATTN_TILE
    row = lax.broadcasted_iota(jnp.int32, (LANES, t), 0)
    lane = lax.broadcasted_iota(jnp.int32, (LANES, t), 1)
    blk = row - HEAD_DIM
    own = i * (t // MOBA_BLOCK) + lane // MOBA_BLOCK
    for hh in range(2):
        qa = q_ref[0, hh * LANES:(hh + 1) * LANES, :]
        km = km_ref[0, hh]
        km_hi = km.astype(BF16)
        km_lo = (km - km_hi.astype(F32)).astype(BF16)
        g = (jnp.dot(km_hi, qa, preferred_element_type=F32)
             + jnp.dot(km_lo, qa, preferred_element_type=F32))
        g = jnp.where((blk >= 0) & (blk < own), g, -jnp.inf)
        sel = blk == own
        for _ in range(MOBA_TOPK):
            mx = jnp.max(g, axis=0, keepdims=True)
            first = jnp.min(jnp.where(g == mx, row, 2 * LANES), axis=0, keepdims=True)
            pick = (row == first) & (mx > -jnp.inf)
            sel = sel | pick
            g = jnp.where(pick, -jnp.inf, g)
        gate = jnp.where(sel, 0.0, NEG).astype(BF16)
        qs[hh] = jnp.where(row < HEAD_DIM, qa, gate)
    _init_softmax_state(m_s, acc_s)

    def scores(j, hh, which):
        k = k_ref[0, pl.ds(pl.multiple_of(j * t, t), t), hh * LANES:(hh + 1) * LANES]
        s = jnp.dot(k, qs[hh], preferred_element_type=F32)
        return s if which is None else s + bias_ref[which, hh]

    def consume(j, hh, st):
        vt = v_ref[0, hh * MOBA_V_ROWS:(hh + 1) * MOBA_V_ROWS, pl.ds(pl.multiple_of(j * t, t), t)]
        _softmax_step(st, vt, m_s, acc_s, hh)

    _causal_sweep(i, scores, consume, s_scr)

    o = jnp.concatenate([_softmax_result(acc_s, 0, HEAD_DIM), _softmax_result(acc_s, 1, HEAD_DIM)],
                        axis=0)
    o_ref[0] = o.astype(BF16)


def _moba_attention(mq_t, mk, mv_t, km, bias):
    b, _, s = mq_t.shape
    t = ATTN_TILE
    return pl.pallas_call(
        _moba_kernel,
        grid=(b, N_MOBA_HEADS // 2, s // t),
        in_specs=[pl.BlockSpec((1, 2 * LANES, t), lambda bi, p, i: (bi, p, i)),
                  pl.BlockSpec((1, s, 2 * LANES), lambda bi, p, i: (bi, 0, p)),
                  pl.BlockSpec((1, 2 * MOBA_V_ROWS, s), lambda bi, p, i: (bi, p, 0)),
                  pl.BlockSpec((1, 2, LANES, LANES), lambda bi, p, i: (bi, p, 0, 0)),
                  pl.BlockSpec((2, 2, t, t), lambda bi, p, i: (0, p, 0, 0))],
        out_specs=pl.BlockSpec((1, LANES, t), lambda bi, p, i: (bi, p, i)),
        out_shape=jax.ShapeDtypeStruct((b, MOBA_COLS, s), BF16),
        scratch_shapes=_attn_scratch(MOBA_V_ROWS),
        compiler_params=_params(3),
        name="moba_attn",
    )(mq_t, mk, mv_t, km, bias)


ROUTER_GROUP_LANE = N_EXPERTS
ROUTE_E1, ROUTE_E2, ROUTE_R1, ROUTE_R2, ROUTE_W1, ROUTE_W2 = range(6)


def _mid_kernel(yd_ref, ym_ref, x_ref, wo_ref, pg_ref, g1_ref, fg_ref, sc2_ref, sh2_ref,
                wr_ref, br_ref, x1_ref, h2_ref, route_ref, tot_ref, run):
    first = (pl.program_id(0) == 0) & (pl.program_id(1) == 0)

    @pl.when(first)
    def _():
        run[...] = jnp.zeros(run.shape, F32)

    tn = (((0,), (0,)), ((), ()))
    y = (lax.dot_general(yd_ref[0], wo_ref[0:DIFF_COLS], tn, preferred_element_type=F32)
         + lax.dot_general(ym_ref[0], wo_ref[DIFF_COLS:D_MODEL], tn, preferred_element_type=F32))
    x1 = x_ref[0] + g1_ref[0] * (_rms(y) * pg_ref[...])
    x1_ref[0] = x1
    h2 = _rms(x1) * fg_ref[...] * (1.0 + sc2_ref[0]) + sh2_ref[0]
    h2_ref[0] = h2

    h_hi = h2.astype(BF16)
    h_lo = (h2 - h_hi.astype(F32)).astype(BF16)
    w_hi = wr_ref[...].astype(BF16)
    w_lo = (wr_ref[...] - w_hi.astype(F32)).astype(BF16)
    logits = (jnp.dot(h_hi, w_hi, preferred_element_type=F32)
              + jnp.dot(h_lo, w_hi, preferred_element_type=F32)
              + jnp.dot(h_hi, w_lo, preferred_element_type=F32)) + br_ref[...]
    lane = lax.broadcasted_iota(jnp.int32, logits.shape, 1)
    is_group = (lane >= ROUTER_GROUP_LANE) & (lane < ROUTER_GROUP_LANE + N_GROUPS)
    gl = jnp.where(is_group, logits, -jnp.inf)
    gmax = jnp.max(gl, axis=1, keepdims=True)
    g_idx = jnp.min(jnp.where(gl == gmax, lane - ROUTER_GROUP_LANE, N_GROUPS), axis=1, keepdims=True)
    g_w = 1.0 / jnp.sum(jnp.exp(gl - gmax), axis=1, keepdims=True)
    in_group = (lane < N_EXPERTS) & (lane // EXPERTS_PER_GROUP == g_idx)
    el = jnp.where(in_group, logits, -jnp.inf)
    m1 = jnp.max(el, axis=1, keepdims=True)
    i1 = jnp.min(jnp.where(el == m1, lane, LANES), axis=1, keepdims=True)
    el2 = jnp.where(lane == i1, -jnp.inf, el)
    m2 = jnp.max(el2, axis=1, keepdims=True)
    i2 = jnp.min(jnp.where(el2 == m2, lane, LANES), axis=1, keepdims=True)
    e2 = jnp.exp(m2 - m1)
    w1 = g_w / (1.0 + e2)
    w2 = g_w * e2 / (1.0 + e2)

    t = MID_TILE
    used = jnp.where((lane == i1) | (lane == i2), 1.0, 0.0)
    earlier = jnp.where(lax.broadcasted_iota(jnp.int32, (t, t), 1)
                        < lax.broadcasted_iota(jnp.int32, (t, t), 0), 1.0, 0.0).astype(BF16)
    rank = jnp.dot(earlier, used.astype(BF16), preferred_element_type=F32) + run[...]
    r1 = jnp.sum(jnp.where(lane == i1, rank, 0.0), axis=1, keepdims=True)
    r2 = jnp.sum(jnp.where(lane == i2, rank, 0.0), axis=1, keepdims=True)
    record = jnp.zeros(logits.shape, F32)
    for field, val in ((ROUTE_E1, i1.astype(F32)), (ROUTE_E2, i2.astype(F32)), (ROUTE_R1, r1),
                       (ROUTE_R2, r2), (ROUTE_W1, w1), (ROUTE_W2, w2)):
        record = jnp.where(lane == field, val, record)
    route_ref[0] = record
    run[...] += jnp.sum(used, axis=0, keepdims=True)
    tot_ref[...] = run[...]


def _mid(yd, ym, x, w_out, post_mix_gain, g1, pre_ffn_gain, sc2, sh2, w_router, b_router):
    b, s, d = x.shape
    t = MID_TILE
    tok = lambda width: pl.BlockSpec((1, t, width), lambda bi, i: (bi, i, 0))
    vec = pl.BlockSpec((1, 1, d), lambda bi, i: (bi, 0, 0))
    row = lambda width: pl.BlockSpec((1, width), lambda bi, i: (0, 0))
    return pl.pallas_call(
        _mid_kernel,
        grid=(b, s // t),
        in_specs=[pl.BlockSpec((1, DIFF_COLS, t), lambda bi, i: (bi, 0, i)),
                  pl.BlockSpec((1, MOBA_COLS, t), lambda bi, i: (bi, 0, i)), tok(d),
                  pl.BlockSpec((d, d), lambda bi, i: (0, 0)),
                  row(d), vec, row(d), vec, vec,
                  pl.BlockSpec((d, LANES), lambda bi, i: (0, 0)), row(LANES)],
        out_specs=[tok(d), tok(d), tok(LANES), row(LANES)],
        out_shape=[jax.ShapeDtypeStruct((b, s, d), F32),
                   jax.ShapeDtypeStruct((b, s, d), F32),
                   jax.ShapeDtypeStruct((b, s, LANES), F32),
                   jax.ShapeDtypeStruct((1, LANES), F32)],
        scratch_shapes=[pltpu.VMEM((1, LANES), F32)],
        compiler_params=_params(2),
        name="out_proj_router",
    )(yd, ym, x, w_out, post_mix_gain, g1, pre_ffn_gain, sc2, sh2, w_router, b_router)


def _row_copy(src, src_row, dst, dst_row, sem):
    return pltpu.make_async_copy(src.at[pl.ds(src_row, 1), :], dst.at[pl.ds(dst_row, 1), :], sem)


def _dispatch_kernel(pos_ref, ends_ref, h_ref, xs_ref, zeros, sem, zero_sem):
    @pl.when(pl.program_id(0) == 0)
    def _():
        zeros[...] = jnp.zeros(zeros.shape, F32)

        def fill(e):
            start = pl.multiple_of(ends_ref[0, e] - EXPERT_TILE, EXPERT_TILE)
            return pltpu.make_async_copy(zeros, xs_ref.at[pl.ds(start, EXPERT_TILE), :], zero_sem)

        def has_rows(e):
            return ends_ref[0, e] > jnp.where(e == 0, 0, ends_ref[0, jnp.maximum(e - 1, 0)])

        def fill_start(e, carry):
            @pl.when(has_rows(e))
            def _():
                fill(e).start()
            return carry

        lax.fori_loop(0, N_EXPERTS, fill_start, 0)

        def fill_wait(e, carry):
            @pl.when(has_rows(e))
            def _():
                fill(e).wait()
            return carry

        lax.fori_loop(0, N_EXPERTS, fill_wait, 0)

        n_used = ends_ref[0, N_EXPERTS - 1] // EXPERT_TILE
        n_tiles = xs_ref.shape[0] // EXPERT_TILE

        def spare(j):
            start = pl.multiple_of(j * EXPERT_TILE, EXPERT_TILE)
            return pltpu.make_async_copy(zeros, xs_ref.at[pl.ds(start, EXPERT_TILE), :], zero_sem)

        def spare_start(j, carry):
            spare(j).start()
            return carry

        lax.fori_loop(n_used, n_tiles, spare_start, 0)

        def spare_wait(j, carry):
            spare(j).wait()
            return carry

        lax.fori_loop(n_used, n_tiles, spare_wait, 0)

    def issue(t, carry):
        for k in range(2):
            _row_copy(h_ref, t, xs_ref, pos_ref[0, 0, 2 * t + k], sem).start()
        return carry

    lax.fori_loop(0, DISPATCH_TILE, issue, 0, unroll=ROW_DMA_UNROLL)

    def drain(t, carry):
        for k in range(2):
            _row_copy(h_ref, t, xs_ref, pos_ref[0, 0, 2 * t + k], sem).wait()
        return carry

    lax.fori_loop(0, DISPATCH_TILE, drain, 0, unroll=ROW_DMA_UNROLL)


def _dispatch(pos, ends, h2, n_rows):
    n, d = h2.shape
    t = DISPATCH_TILE
    return pl.pallas_call(
        _dispatch_kernel,
        grid=(n // t,),
        in_specs=[pl.BlockSpec((1, 1, 2 * t), lambda i: (i, 0, 0), memory_space=pltpu.SMEM),
                  pl.BlockSpec(memory_space=pltpu.SMEM),
                  pl.BlockSpec((t, d), lambda i: (i, 0))],
        out_specs=pl.BlockSpec(memory_space=pl.ANY),
        out_shape=jax.ShapeDtypeStruct((n_rows, d), F32),
        scratch_shapes=[pltpu.VMEM((EXPERT_TILE, d), F32), pltpu.SemaphoreType.DMA(()),
                        pltpu.SemaphoreType.DMA(())],
        compiler_params=_params(1),
        name="moe_dispatch",
    )(pos.reshape(n // t, 1, 2 * t), ends[None], h2)


def _expert_kernel(te_ref, x_ref, wg_ref, wu_ref, wd_ref, y_ref, wg_s, wu_s, wd_s):
    j = pl.program_id(0)
    n_used = te_ref[pl.num_programs(0)]

    @pl.when(j >= n_used)
    def _():
        y_ref[...] = jnp.zeros(y_ref.shape, F32)

    @pl.when(j < n_used)
    def _():
        new_expert = (j == 0) | (te_ref[j] != te_ref[jnp.maximum(j - 1, 0)])

        @pl.when(new_expert)
        def _():
            wg_s[...] = wg_ref[0].astype(BF16)
            wu_s[...] = wu_ref[0].astype(BF16)
            wd_s[...] = wd_ref[0].astype(BF16)

        x = x_ref[...].astype(BF16)
        a = jnp.dot(x, wg_s[...], preferred_element_type=F32)
        u = jnp.dot(x, wu_s[...], preferred_element_type=F32)
        hid = (a * jax.nn.sigmoid(a)) * u
        y_ref[...] = jnp.dot(hid.astype(BF16), wd_s[...], preferred_element_type=F32)


def _experts(tile_expert, xs, w_gate, w_up, w_down):
    rows, d = xs.shape
    t = EXPERT_TILE
    n_tiles = rows // t
    return pl.pallas_call(
        _expert_kernel,
        grid_spec=pltpu.PrefetchScalarGridSpec(
            num_scalar_prefetch=1,
            grid=(n_tiles,),
            in_specs=[pl.BlockSpec((t, d), lambda j, te: (jnp.minimum(j, te[n_tiles] - 1), 0)),
                      pl.BlockSpec((1, d, EXPERT_FF), lambda j, te: (te[j], 0, 0)),
                      pl.BlockSpec((1, d, EXPERT_FF), lambda j, te: (te[j], 0, 0)),
                      pl.BlockSpec((1, EXPERT_FF, d), lambda j, te: (te[j], 0, 0))],
            out_specs=pl.BlockSpec((t, d), lambda j, te: (j, 0)),
            scratch_shapes=[pltpu.VMEM((d, EXPERT_FF), BF16), pltpu.VMEM((d, EXPERT_FF), BF16),
                            pltpu.VMEM((EXPERT_FF, d), BF16)]),
        out_shape=jax.ShapeDtypeStruct((rows, d), F32),
        compiler_params=_params(1),
        name="routed_experts",
    )(tile_expert, xs, w_gate, w_up, w_down)


def _combine_kernel(pos_ref, y_ref, route_ref, x1_ref, g2_ref, pg_ref, o_ref, buf, sem):
    t = COMBINE_TILE

    def issue(r, carry):
        for k in range(2):
            _row_copy(y_ref, pos_ref[0, 0, 2 * r + k], buf.at[k], r, sem).start()
        return carry

    lax.fori_loop(0, t, issue, 0, unroll=ROW_DMA_UNROLL)

    def drain(r, carry):
        for k in range(2):
            _row_copy(y_ref, pos_ref[0, 0, 2 * r + k], buf.at[k], r, sem).wait()
        return carry

    lax.fori_loop(0, t, drain, 0, unroll=ROW_DMA_UNROLL)

    route = route_ref[0]
    w1 = route[:, ROUTE_W1:ROUTE_W1 + 1]
    w2 = route[:, ROUTE_W2:ROUTE_W2 + 1]
    y = w1 * buf[0] + w2 * buf[1]
    o_ref[0] = x1_ref[0] + g2_ref[0] * (_rms(y) * pg_ref[...])


def _combine(pos, ys, route, x1, g2, post_ffn_gain):
    b, s, d = x1.shape
    t = COMBINE_TILE
    nt = s // t
    tok = lambda width: pl.BlockSpec((1, t, width), lambda bi, i: (bi, i, 0))
    return pl.pallas_call(
        _combine_kernel,
        grid=(b, nt),
        in_specs=[pl.BlockSpec((1, 1, 2 * t), lambda bi, i: (bi * nt + i, 0, 0),
                               memory_space=pltpu.SMEM),
                  pl.BlockSpec(memory_space=pl.ANY),
                  tok(LANES), tok(d),
                  pl.BlockSpec((1, 1, d), lambda bi, i: (bi, 0, 0)),
                  pl.BlockSpec((1, d), lambda bi, i: (0, 0))],
        out_specs=tok(d),
        out_shape=jax.ShapeDtypeStruct((b, s, d), F32),
        scratch_shapes=[pltpu.VMEM((2, t, d), F32), pltpu.SemaphoreType.DMA(())],
        compiler_params=_params(2),
        name="moe_combine",
    )(pos.reshape(b * nt, 1, 2 * t), ys, route, x1, g2, post_ffn_gain)


def _routing_tables(route, totals, n_rows):
    t = EXPERT_TILE
    tot = totals[0, :N_EXPERTS].astype(jnp.int32)
    padded = (tot + t - 1) // t * t
    ends = jnp.cumsum(padded)
    base = ends - padded
    rec = route.reshape(-1, LANES)
    e = rec[:, ROUTE_E1:ROUTE_E2 + 1].astype(jnp.int32)
    r = rec[:, ROUTE_R1:ROUTE_R2 + 1].astype(jnp.int32)
    pos = base[e] + r
    tile_start = jnp.arange(n_rows // t, dtype=jnp.int32) * t
    tile_expert = jnp.minimum(jnp.sum(tile_start[:, None] >= ends[None, :], axis=1),
                              N_EXPERTS - 1).astype(jnp.int32)
    tiles_used = ends[-1:] // t
    return pos.reshape(-1), ends, jnp.concatenate([tile_expert, tiles_used])


def _augment_w_in(w_in):
    d = w_in.shape[0]
    scale = HEAD_DIM ** -0.5 * LOG2E
    dq, dk, dv, mq, mk, mv = jnp.split(w_in, [512, 1024, 1536, 2048, 2560], axis=1)

    def pad_heads(w):
        w = w.reshape(d, N_MOBA_HEADS, HEAD_DIM)
        return jnp.pad(w, ((0, 0), (0, 0), (0, LANES - HEAD_DIM))).reshape(d, N_MOBA_HEADS * LANES)

    return jnp.concatenate([dq * scale, dk, dv, pad_heads(mq * scale), pad_heads(mk), mv],
                           axis=1).astype(BF16)


def kernel(x, c, w_ada, b_ada, pre_mix_gain, post_mix_gain, pre_ffn_gain, post_ffn_gain, w_in, lambda_q1, lambda_k1, lambda_q2, lambda_k2, diff_head_gain, w_out, rel_bias, w_group, b_group, w_expert, b_expert, w_gate, w_up, w_down):
    b, s, d = x.shape
    depth = w_in.shape[0]
    tab_t = rel_bias.T
    bias_diff = _bias_tiles(tab_t, ATTN_TILE, 0, N_DIFF_HEADS)
    bias_moba = _bias_tiles(tab_t, ATTN_TILE, N_DIFF_HEADS, N_MOBA_HEADS)
    c_pad = jnp.pad(c, ((0, 8 - b), (0, 0)))
    for l in range(depth):
        lambda_init = 0.8 - 0.6 * math.exp(-0.3 * l)
        mod = _ada(c_pad, w_ada[l], b_ada[l][None])[:b]
        sh1, sc1, g1, sh2, sc2, g2 = [m[:, None, :] for m in jnp.split(mod, 6, axis=-1)]

        dq, dk, dv, mq, mk, mv, kmean = _inproj(x, pre_mix_gain[l][None], sc1, sh1,
                                                _augment_w_in(w_in[l]))
        y_diff = _diff_attention(dq, dk, dv, bias_diff, lambda_q1[l][:, None], lambda_k1[l][:, None],
                                 lambda_q2[l][:, None], lambda_k2[l][:, None],
                                 diff_head_gain[l][:, None], lambda_init)
        nkb = s // MOBA_BLOCK
        km = kmean.reshape(b, nkb, N_MOBA_HEADS, LANES)[..., :HEAD_DIM]
        km = jnp.pad(km.transpose(0, 2, 1, 3),
                     ((0, 0), (0, 0), (HEAD_DIM, LANES - HEAD_DIM - nkb), (0, LANES - HEAD_DIM)))
        y_moba = _moba_attention(mq, mk, mv, km, bias_moba)

        w_router = jnp.pad(jnp.concatenate([w_expert[l], w_group[l]], axis=1),
                           ((0, 0), (0, LANES - N_EXPERTS - N_GROUPS)))
        b_router = jnp.pad(jnp.concatenate([b_expert[l], b_group[l]]),
                           (0, LANES - N_EXPERTS - N_GROUPS))[None]
        x1, h2, route, totals = _mid(
            y_diff, y_moba, x, w_out[l].astype(BF16), post_mix_gain[l][None], g1,
            pre_ffn_gain[l][None], sc2, sh2, w_router, b_router)
        n_rows = 2 * b * s + N_EXPERTS * EXPERT_TILE
        pos, ends, tile_expert = _routing_tables(route, totals, n_rows)
        xs = _dispatch(pos, ends, h2.reshape(b * s, d), n_rows)
        ys = _experts(tile_expert, xs, w_gate[l], w_up[l], w_down[l])
        x = _combine(pos, ys, route, x1, g2, post_ffn_gain[l][None])
    return x
```
